```python
import math
import jax
import jax.numpy as jnp
from jax import lax
import numpy as np

D_MODEL = 1024
BATCH = 8
SEQ = 4096
DEPTH = 2

BRANCH_W = D_MODEL // 2
N_BRANCH = 3
NSA_HEADS = 8
NSA_GROUPS = 2
NSA_HPG = NSA_HEADS // NSA_GROUPS
NSA_DH = BRANCH_W // NSA_HEADS
NSA_KV = NSA_GROUPS * NSA_DH
CMP_BLOCK = 32
CMP_STRIDE = 16
CMP_HIDDEN = 2 * NSA_DH
SEL_BLOCK = 64
SEL_TOPK = 8
WINDOW = 512
Q_BLOCK = 128
RET_HEADS = 4
RET_DV = BRANCH_W // RET_HEADS
RET_DK = RET_DV // 2
RET_CHUNK = 128
ROPE_BASE = 10000.0
CONV_CH = BRANCH_W
CONV_WIDTH = 31
REL_BUCKETS = 32
REL_MAX_DIST = 128
D_FF = 4 * D_MODEL
EPS = 1e-6
IN_WIDTHS = (NSA_HEADS * NSA_DH, NSA_KV, NSA_KV, NSA_KV, NSA_KV, NSA_KV, NSA_KV, 3 * NSA_HEADS,
             RET_HEADS * RET_DK, RET_HEADS * RET_DK, RET_HEADS * RET_DV, RET_HEADS * RET_DV,
             CONV_CH, CONV_CH, N_BRANCH * D_MODEL)
IN_TOTAL = sum(IN_WIDTHS)

kernel_name = 'hybrid_nsa_retention_conformer_block'


def _rmsnorm(x, g):
    x32 = x.astype(jnp.float32)
    y = x32 * lax.rsqrt(jnp.mean(x32 * x32, axis=-1, keepdims=True) + EPS)
    return (y * g.astype(jnp.float32)).astype(x.dtype)


def _layernorm(x, g, b=None):
    x32 = x.astype(jnp.float32)
    mu = jnp.mean(x32, axis=-1, keepdims=True)
    var = jnp.mean(jnp.square(x32 - mu), axis=-1, keepdims=True)
    y = (x32 - mu) * lax.rsqrt(var + EPS) * g.astype(jnp.float32)
    if b is not None:
        y = y + b.astype(jnp.float32)
    return y


def _split_cols(z):
    parts = []
    start = 0
    for w in IN_WIDTHS:
        parts.append(z[..., start:start + w])
        start += w
    return parts


def _rel_bucket(dist):
    n = jnp.maximum(dist, 0)
    max_exact = REL_BUCKETS // 2
    nf = jnp.maximum(n, 1).astype(jnp.float32)
    large = max_exact + (jnp.log(nf / max_exact) / math.log(REL_MAX_DIST / max_exact)
                         * (REL_BUCKETS - max_exact)).astype(jnp.int32)
    large = jnp.minimum(large, REL_BUCKETS - 1)
    return jnp.where(n < max_exact, n, large)


def _masked_softmax(s, mask):
    s = jnp.where(mask, s, -1e30)
    m = jnp.max(s, axis=-1, keepdims=True)
    p = jnp.exp(s - m) * mask
    den = jnp.sum(p, axis=-1, keepdims=True)
    return p / jnp.maximum(den, 1e-30)


def _rotary(x, pos):
    half = x.shape[-1] // 2
    inv = ROPE_BASE ** (-jnp.arange(half, dtype=jnp.float32) / half)
    ang = pos[:, None] * inv[None, :]
    cos = jnp.cos(ang)[:, None, :]
    sin = jnp.sin(ang)[:, None, :]
    x32 = x.astype(jnp.float32)
    x1, x2 = x32[..., :half], x32[..., half:]
    return jnp.concatenate([x1 * cos - x2 * sin, x1 * sin + x2 * cos], axis=-1).astype(x.dtype)


def _compress(x, pe, w1, w2):
    B, S, G, DH = x.shape
    n_ch = S // CMP_STRIDE
    r = CMP_BLOCK // CMP_STRIDE
    n_c = n_ch - r + 1
    ch = x.reshape(B, n_ch, CMP_STRIDE, G, DH)
    blocks = jnp.concatenate([ch[:, i:i + n_c] for i in range(r)], axis=2) + pe[None, None, :, None, :]
    hid = jax.nn.gelu(jnp.einsum('bnlgd,ldf->bngf', blocks, w1))
    return jnp.einsum('bngf,fd->bngd', hid, w2)


def _nsa(q, kc, vc, ks, vs, kw, vw, gl, pe_k, w1_k, w2_k, pe_v, w1_v, w2_v, rel_table):
    B, S, _ = q.shape
    G, HPG, DH = NSA_GROUPS, NSA_HPG, NSA_DH
    dt = q.dtype
    q = q.reshape(B, S, G, HPG, DH) * (DH ** -0.5)
    kc, vc, ks, vs, kw, vw = [a.reshape(B, S, G, DH) for a in (kc, vc, ks, vs, kw, vw)]
    k_cmp = _compress(kc, pe_k, w1_k, w2_k)
    v_cmp = _compress(vc, pe_v, w1_v, w2_v)
    n_c = k_cmp.shape[1]
    cmp_start = jnp.arange(n_c) * CMP_STRIDE
    cmp_end = cmp_start + (CMP_BLOCK - 1)
    n_s = S // SEL_BLOCK
    k_eff = min(SEL_TOPK, n_s)
    sel_start = jnp.arange(n_s) * SEL_BLOCK
    overlap = ((cmp_start[:, None] <= sel_start[None, :] + SEL_BLOCK - 1)
               & (cmp_end[:, None] >= sel_start[None, :])).astype(jnp.float32)
    ks_blk = ks.reshape(B, n_s, SEL_BLOCK, G, DH).transpose(0, 3, 1, 2, 4)
    vs_blk = vs.reshape(B, n_s, SEL_BLOCK, G, DH).transpose(0, 3, 1, 2, 4)
    kw_pad = jnp.pad(kw, ((0, 0), (WINDOW, 0), (0, 0), (0, 0)))
    vw_pad = jnp.pad(vw, ((0, 0), (WINDOW, 0), (0, 0), (0, 0)))
    tb = rel_table.reshape(REL_BUCKETS, G, HPG).transpose(1, 0, 2)
    b_idx = jnp.arange(B)[:, None, None, None]
    g_idx = jnp.arange(G)[None, :, None, None]
    sb_ar = jnp.arange(SEL_BLOCK)

    def head_bias(dist):
        bias = rel_table[_rel_bucket(dist)]
        return jnp.transpose(bias, (2, 0, 1)).reshape(G, HPG, dist.shape[0], dist.shape[1]).astype(jnp.float32)

    def block_fn(args):
        c, qb, gb = args
        t = c * Q_BLOCK + jnp.arange(Q_BLOCK)
        dist_c = t[:, None] - cmp_end[None, :]
        s_c = jnp.einsum('bqghd,bngd->bghqn', qb, k_cmp).astype(jnp.float32) + head_bias(dist_c)
        p_c = _masked_softmax(s_c, dist_c >= 0)
        o_c = jnp.einsum('bghqn,bngd->bqghd', p_c, v_cmp)
        imp = jnp.einsum('bghqn,ns->bgqs', p_c, overlap)
        jb = jnp.arange(n_s)[None, :]
        cur = (t // SEL_BLOCK)[:, None]
        valid = jb <= cur
        forced = (jb == 0) | (jb == cur) | (jb == cur - 1)
        score = jnp.where(forced, imp + 1e4, jnp.where(valid, imp, -1e4))
        _, idx = lax.top_k(score, k_eff)
        kg = ks_blk[b_idx, g_idx, idx]
        vg = vs_blk[b_idx, g_idx, idx]
        pos_s = idx[..., None] * SEL_BLOCK + sb_ar
        dist_s = t[None, None, :, None, None] - pos_s
        bias_s = tb[g_idx[..., None], _rel_bucket(dist_s)]
        s_s = jnp.einsum('bqghd,bgqkld->bghqkl', qb, kg).astype(jnp.float32) + jnp.moveaxis(bias_s, -1, 2).astype(jnp.float32)
        mask_s = (dist_s >= 0)[:, :, None]
        p_s = _masked_softmax(s_s.reshape(B, G, HPG, Q_BLOCK, k_eff * SEL_BLOCK),
                              mask_s.reshape(B, G, 1, Q_BLOCK, k_eff * SEL_BLOCK))
        p_s = p_s.reshape(B, G, HPG, Q_BLOCK, k_eff, SEL_BLOCK)
        o_s = jnp.einsum('bghqkl,bgqkld->bqghd', p_s, vg)
        start = c * Q_BLOCK
        kwb = lax.dynamic_slice_in_dim(kw_pad, start, WINDOW + Q_BLOCK, axis=1)
        vwb = lax.dynamic_slice_in_dim(vw_pad, start, WINDOW + Q_BLOCK, axis=1)
        pos_w = start - WINDOW + jnp.arange(WINDOW + Q_BLOCK)
        dist_w = t[:, None] - pos_w[None, :]
        mask_w = (dist_w >= 0) & (dist_w < WINDOW) & (pos_w[None, :] >= 0)
        s_w = jnp.einsum('bqghd,bkgd->bghqk', qb, kwb).astype(jnp.float32) + head_bias(dist_w)
        p_w = _masked_softmax(s_w, mask_w)
        o_w = jnp.einsum('bghqk,bkgd->bqghd', p_w, vwb)
        gs = jax.nn.sigmoid(gb.astype(jnp.float32)).reshape(B, Q_BLOCK, G, HPG, 3)
        o = gs[..., 0:1] * o_c + gs[..., 1:2] * o_s + gs[..., 2:3] * o_w
        return o.astype(dt)

    n_qb = S // Q_BLOCK
    q_blocks = q.reshape(B, n_qb, Q_BLOCK, G, HPG, DH).transpose(1, 0, 2, 3, 4, 5)
    g_blocks = gl.reshape(B, n_qb, Q_BLOCK, NSA_HEADS, 3).transpose(1, 0, 2, 3, 4)
    out = lax.map(block_fn, (jnp.arange(n_qb), q_blocks, g_blocks))
    return out.transpose(1, 0, 2, 3, 4, 5).reshape(B, S, NSA_HEADS * DH)


def _retention(q, k, v, g, gn_gain):
    B, S, _ = q.shape
    H, DK, DV, C = RET_HEADS, RET_DK, RET_DV, RET_CHUNK
    dt = q.dtype
    pos = jnp.arange(S, dtype=jnp.float32)
    q = _rotary(q.reshape(B, S, H, DK), pos)
    k = _rotary(k.reshape(B, S, H, DK), pos) * (DK ** -0.5)
    v = v.reshape(B, S, H, DV)
    log_g = jnp.log1p(-jnp.exp2(-5.0 - jnp.arange(H, dtype=jnp.float32)))
    N = S // C
    qc = q.reshape(B, N, C, H, DK)
    kc = k.reshape(B, N, C, H, DK)
    vc = v.reshape(B, N, C, H, DV)
    ar = jnp.arange(C)
    diff = ar[:, None] - ar[None, :]
    decay = jnp.where(diff >= 0, jnp.exp(log_g[:, None, None] * jnp.maximum(diff, 0).astype(jnp.float32)), 0.0)
    inner = jnp.einsum('bnchd,bnehd->bnhce', qc, kc).astype(jnp.float32) * decay
    o_inner = jnp.einsum('bnhce,bnehv->bnchv', inner, vc)
    zeta = jnp.exp(log_g[:, None] * (C - 1 - ar).astype(jnp.float32))
    xi = jnp.exp(log_g[:, None] * (ar + 1).astype(jnp.float32))
    kv = jnp.einsum('bnchd,hc,bnchv->bnhdv', kc, zeta, vc)
    g_chunk = jnp.exp(log_g * C)[None, :, None, None]

    def step(state, kv_n):
        return g_chunk * state + kv_n, state

    init = jnp.zeros((B, H, DK, DV), kv.dtype)
    _, prev = lax.scan(step, init, kv.transpose(1, 0, 2, 3, 4))
    prev = prev.transpose(1, 0, 2, 3, 4)
    o_cross = jnp.einsum('bnchd,hc,bnhdv->bnchv', qc, xi, prev)
    o = (o_inner + o_cross).reshape(B, S, H, DV)
    o = _layernorm(o, jnp.ones((DV,), jnp.float32)).reshape(B, S, H * DV) * gn_gain.astype(jnp.float32)
    return (jax.nn.silu(g.astype(jnp.float32)) * o).astype(dt)


def _conv_module(a, b, dw_w, dw_b, ln_g, ln_b):
    u = a * jax.nn.sigmoid(b)
    kern = dw_w[:, None, :].astype(u.dtype)
    y = lax.conv_general_dilated(u, kern, window_strides=(1,), padding=[(CONV_WIDTH - 1, 0)],
                                 dimension_numbers=('NWC', 'WIO', 'NWC'), feature_group_count=CONV_CH)
    y = y + dw_b
    y = _layernorm(y, ln_g, ln_b)
    return jax.nn.silu(y).astype(a.dtype)


def setup_inputs(seed: int = 0) -> dict:
    key = jax.random.key(seed)
    k = jax.random.split(key, 21)
    f32 = jnp.float32

    def nrm(kk, shape, scale):
        return jax.random.normal(kk, shape, f32) * scale

    return {
        'x': nrm(k[0], (BATCH, SEQ, D_MODEL), 1.0),
        'rel_table': nrm(k[1], (REL_BUCKETS, NSA_HEADS), 0.5),
        'norm_mix': 1.0 + nrm(k[2], (DEPTH, D_MODEL), 0.05),
        'w_in': nrm(k[3], (DEPTH, D_MODEL, IN_TOTAL), D_MODEL ** -0.5),
        'cmp_pe_k': nrm(k[4], (DEPTH, CMP_BLOCK, NSA_DH), 0.5),
        'cmp_w1_k': nrm(k[5], (DEPTH, CMP_BLOCK, NSA_DH, CMP_HIDDEN), (CMP_BLOCK * NSA_DH) ** -0.5),
        'cmp_w2_k': nrm(k[6], (DEPTH, CMP_HIDDEN, NSA_DH), CMP_HIDDEN ** -0.5),
        'cmp_pe_v': nrm(k[7], (DEPTH, CMP_BLOCK, NSA_DH), 0.5),
        'cmp_w1_v': nrm(k[8], (DEPTH, CMP_BLOCK, NSA_DH, CMP_HIDDEN), (CMP_BLOCK * NSA_DH) ** -0.5),
        'cmp_w2_v': nrm(k[9], (DEPTH, CMP_HIDDEN, NSA_DH), CMP_HIDDEN ** -0.5),
        'ret_gn': 1.0 + nrm(k[10], (DEPTH, RET_HEADS * RET_DV), 0.05),
        'conv_w': nrm(k[11], (DEPTH, CONV_WIDTH, CONV_CH), CONV_WIDTH ** -0.5),
        'conv_b': nrm(k[12], (DEPTH, CONV_CH), 0.02),
        'conv_ln_g': 1.0 + nrm(k[13], (DEPTH, CONV_CH), 0.05),
        'conv_ln_b': nrm(k[14], (DEPTH, CONV_CH), 0.02),
        'w_branch': nrm(k[15], (DEPTH, N_BRANCH, BRANCH_W, D_MODEL), BRANCH_W ** -0.5),
        'w_out': nrm(k[16], (DEPTH, D_MODEL, D_MODEL), D_MODEL ** -0.5),
        'norm_mlp': 1.0 + nrm(k[17], (DEPTH, D_MODEL), 0.05),
        'w_ff1': nrm(k[18], (DEPTH, D_MODEL, D_FF), D_MODEL ** -0.5),
        'w_ff2': nrm(k[19], (DEPTH, D_FF, D_MODEL), D_FF ** -0.5),
        'norm_final': 1.0 + nrm(k[20], (D_MODEL,), 0.05),
    }


def reference(x, rel_table, norm_mix, w_in, cmp_pe_k, cmp_w1_k, cmp_w2_k, cmp_pe_v, cmp_w1_v, cmp_w2_v,
              ret_gn, conv_w, conv_b, conv_ln_g, conv_ln_b, w_branch, w_out, norm_mlp, w_ff1, w_ff2,
              norm_final):
    B, S, _ = x.shape
    for l in range(DEPTH):
        h = _rmsnorm(x, norm_mix[l])
        z = h @ w_in[l]
        (q_n, kc, vc, ks, vs, kw, vw, g_n, q_r, k_r, v_r, g_r, c_a, c_b, m_g) = _split_cols(z)
        o_nsa = _nsa(q_n, kc, vc, ks, vs, kw, vw, g_n, cmp_pe_k[l], cmp_w1_k[l], cmp_w2_k[l],
                     cmp_pe_v[l], cmp_w1_v[l], cmp_w2_v[l], rel_table)
        o_ret = _retention(q_r, k_r, v_r, g_r, ret_gn[l])
        o_conv = _conv_module(c_a, c_b, conv_w[l], conv_b[l], conv_ln_g[l], conv_ln_b[l])
        br = jnp.stack([o_nsa.astype(x.dtype), o_ret.astype(x.dtype), o_conv.astype(x.dtype)], axis=2)
        proj = jnp.einsum('bsnc,ncd->bsnd', br, w_branch[l])
        gate = jax.nn.sigmoid(m_g.astype(jnp.float32)).reshape(B, S, N_BRANCH, D_MODEL)
        merged = jnp.sum(gate * proj, axis=2).astype(x.dtype)
        x = x + merged @ w_out[l]
        h2 = _rmsnorm(x, norm_mlp[l])
        x = x + jnp.square(jax.nn.relu(h2 @ w_ff1[l])) @ w_ff2[l]
    return _rmsnorm(x, norm_final)
```

```python
import functools
import math

import jax
import jax.numpy as jnp
import numpy as np
from jax import lax
from jax.experimental import pallas as pl
from jax.experimental.pallas import tpu as pltpu

F32 = jnp.float32
BF16 = jnp.bfloat16

D_MODEL = 1024
BRANCH_W = 512
N_BRANCH = 3
NSA_HEADS = 8
NSA_GROUPS = 2
NSA_HPG = 4
NSA_DH = 64
NSA_KV = NSA_GROUPS * NSA_DH
CMP_BLOCK = 32
CMP_STRIDE = 16
CMP_HIDDEN = 128
SEL_BLOCK = 64
SEL_TOPK = 8
WINDOW = 512
Q_BLOCK = 128
RET_HEADS = 4
RET_DV = 128
RET_DK = 64
RET_CHUNK = 128
ROPE_BASE = 10000.0
CONV_WIDTH = 31
REL_BUCKETS = 32
REL_MAX_DIST = 128
D_FF = 4096
EPS = 1e-6
NEG = -1e30

LANE = 128
VMEM_LIMIT = 56 * 1024 * 1024

COL_MG = 0
COL_QN = 3072
COL_VR = 3584
COL_GR = 4096
COL_CA = 4608
COL_CB = 5120
COL_QR = 5632
COL_KR = 5888
COL_KC = 6144
COL_GN = 6912
Z_COLS = 7040


def _cparams(sem):
    return pltpu.CompilerParams(dimension_semantics=sem, vmem_limit_bytes=VMEM_LIMIT)


def _in_proj_body(x_ref, g_ref, w_ref, z_ref, h_scr):
    @pl.when(pl.program_id(1) == 0)
    def _():
        x = x_ref[...]
        ms = jnp.mean(x * x, axis=-1, keepdims=True)
        h_scr[...] = (x * lax.rsqrt(ms + EPS) * g_ref[...]).astype(BF16)

    z_ref[...] = jnp.dot(h_scr[...], w_ref[...], preferred_element_type=F32)


def _in_proj(x2, g, w):
    T = x2.shape[0]
    tm, tn = 512, 1408
    return pl.pallas_call(
        _in_proj_body,
        grid=(T // tm, Z_COLS // tn),
        in_specs=[pl.BlockSpec((tm, D_MODEL), lambda i, j: (i, 0)),
                  pl.BlockSpec((1, D_MODEL), lambda i, j: (0, 0)),
                  pl.BlockSpec((D_MODEL, tn), lambda i, j: (0, j))],
        out_specs=pl.BlockSpec((tm, tn), lambda i, j: (i, j)),
        out_shape=jax.ShapeDtypeStruct((T, Z_COLS), F32),
        scratch_shapes=[pltpu.VMEM((tm, D_MODEL), BF16)],
        compiler_params=_cparams(("parallel", "arbitrary")),
        name="in_proj",
    )(x2, g, w)


def _compress_body(kc_ref, vc_ref, pek_ref, w1k_ref, w2k_ref, pev_ref, w1v_ref, w2v_ref, ko_ref, vo_ref):
    def one(c_ref, pe_ref, w1_ref, w2_ref, o_ref):
        ch = c_ref[...]
        a = jnp.dot((ch + pe_ref[0:1, :]).astype(BF16), w1_ref[0], preferred_element_type=F32)
        b = jnp.dot((ch + pe_ref[1:2, :]).astype(BF16), w1_ref[1], preferred_element_type=F32)
        n = b.shape[0]
        hid = jax.nn.gelu(a + pltpu.roll(b, n - 1, 0))
        o_ref[...] = jnp.dot(hid.astype(BF16), w2_ref[...], preferred_element_type=F32).astype(o_ref.dtype)

    one(kc_ref, pek_ref, w1k_ref, w2k_ref, ko_ref)
    one(vc_ref, pev_ref, w1v_ref, w2v_ref, vo_ref)


def _compress(kc2, vc2, pek, w1k, w2k, pev, w1v, w2v):
    B, n_ch, wide = kc2.shape
    hid2 = NSA_GROUPS * CMP_HIDDEN
    act = pl.BlockSpec((None, n_ch, wide), lambda b: (b, 0, 0))
    pe = pl.BlockSpec((2, wide), lambda b: (0, 0))
    w1 = pl.BlockSpec((2, wide, hid2), lambda b: (0, 0, 0))
    w2 = pl.BlockSpec((hid2, NSA_KV), lambda b: (0, 0))
    out = pl.BlockSpec((None, n_ch, NSA_KV), lambda b: (b, 0, 0))
    return pl.pallas_call(
        _compress_body,
        grid=(B,),
        in_specs=[act, act, pe, w1, w2, pe, w1, w2],
        out_specs=[out, out],
        out_shape=[jax.ShapeDtypeStruct((B, n_ch, NSA_KV), BF16)] * 2,
        compiler_params=_cparams(("parallel",)),
        name="nsa_compress",
    )(kc2, vc2, pek, w1k, w2k, pev, w1v, w2v)


def _softmax_rows(s):
    m = jnp.max(s, axis=-1, keepdims=True)
    p = jnp.where(s > 0.5 * NEG, jnp.exp(s - m), 0.0)
    den = jnp.sum(p, axis=-1, keepdims=True)
    return p / jnp.maximum(den, 1e-30)


def _nsa_body(q_ref, gn_ref, ks_ref, vs_ref, kw_ref, vw_ref, kcmp_ref, vcmp_ref,
              et_ref, ov_ref, bnear_ref, bfar_ref, bwin_ref, bcmp_ref, o_ref,
              kaug_scr, vs_scr, kw_scr, vw_scr):
    c = pl.program_id(1)
    n_s = et_ref.shape[1]
    n_cp = kcmp_ref.shape[0]
    QB, DH, HPG = Q_BLOCK, NSA_DH, NSA_HPG

    @pl.when(c == 0)
    def _():
        for g in range(NSA_GROUPS):
            kaug_scr[g, :, 0:DH] = ks_ref[:, g * DH:(g + 1) * DH].astype(BF16)
            kaug_scr[g, :, DH:DH + n_s] = et_ref[...]
        vs_scr[...] = vs_ref[...].astype(BF16)
        kw_scr[...] = kw_ref[...].astype(BF16)
        vw_scr[...] = vw_ref[...].astype(BF16)

    t0 = c * QB
    q = q_ref[...].astype(BF16)
    gates = jax.nn.sigmoid(gn_ref[...])
    nt = (((1,), (1,)), ((), ()))

    row_t = t0 + lax.broadcasted_iota(jnp.int32, (QB, n_cp), 0)
    cmp_end = lax.broadcasted_iota(jnp.int32, (QB, n_cp), 1) * CMP_STRIDE + (CMP_BLOCK - 1)
    cmp_ok1 = row_t >= cmp_end
    cmp_ok = jnp.concatenate([cmp_ok1] * HPG, axis=0)
    cmp_shift = lax.rem(c * (QB // CMP_STRIDE) + n_cp // 2, n_cp)

    lane_s = lax.broadcasted_iota(jnp.int32, (QB, n_s), 1)
    cur = (t0 + lax.broadcasted_iota(jnp.int32, (QB, n_s), 0)) // SEL_BLOCK
    valid = lane_s <= cur
    forced = (lane_s == 0) | (lane_s == cur) | (lane_s == cur - 1)
    lane_sf = lane_s.astype(F32)

    for g in range(NSA_GROUPS):
        qs = jnp.concatenate([q[:, (g * HPG + h) * DH:(g * HPG + h + 1) * DH] for h in range(HPG)], axis=0)

        kc_g = kcmp_ref[:, g * DH:(g + 1) * DH]
        vc_g = vcmp_ref[:, g * DH:(g + 1) * DH]
        s_c = lax.dot_general(qs, kc_g, nt, preferred_element_type=F32)
        b_c = jnp.concatenate([pltpu.roll(bcmp_ref[g * HPG + h], cmp_shift, 1) for h in range(HPG)], axis=0)
        p_c = _softmax_rows(jnp.where(cmp_ok, s_c + b_c, NEG))
        o_c = jnp.dot(p_c.astype(BF16), vc_g, preferred_element_type=F32)

        p_sum = p_c[0:QB] + p_c[QB:2 * QB] + p_c[2 * QB:3 * QB] + p_c[3 * QB:4 * QB]
        p_hi = p_sum.astype(BF16)
        p_lo = (p_sum - p_hi.astype(F32)).astype(BF16)
        imp = (jnp.dot(p_hi, ov_ref[...], preferred_element_type=F32)
               + jnp.dot(p_lo, ov_ref[...], preferred_element_type=F32))
        score = jnp.where(forced, imp + 1e4, jnp.where(valid, imp, -1e4))
        selneg = jnp.full((QB, n_s), NEG, F32)
        for _ in range(min(SEL_TOPK, n_s)):
            m = jnp.max(score, axis=-1, keepdims=True)
            first = jnp.min(jnp.where(score == m, lane_sf, float(n_s)), axis=-1, keepdims=True)
            pick = lane_sf == first
            selneg = jnp.where(pick, 0.0, selneg)
            score = jnp.where(pick, -jnp.inf, score)
        selneg = selneg.astype(BF16)

        qa = jnp.concatenate([qs, jnp.concatenate([selneg] * HPG, axis=0)], axis=1)
        b_far = bfar_ref[g * HPG:(g + 1) * HPG].reshape(HPG * QB, QB)

        def sel_tile(carry, kt, bias):
            m_i, l_i, acc = carry
            r0 = pl.multiple_of(kt * QB, QB)
            kb = kaug_scr[g, pl.ds(r0, QB), :]
            vb = vs_scr[pl.ds(r0, QB), g * DH:(g + 1) * DH]
            s = lax.dot_general(qa, kb, nt, preferred_element_type=F32) + bias
            m_n = jnp.maximum(m_i, jnp.max(s, axis=-1, keepdims=True))
            alpha = jnp.exp(m_i - m_n)
            p = jnp.exp(s - m_n)
            l_n = alpha * l_i + jnp.sum(p, axis=-1, keepdims=True)
            acc_n = alpha * acc + jnp.dot(p.astype(BF16), vb, preferred_element_type=F32)
            return m_n, l_n, acc_n

        init = (jnp.full((HPG * QB, 1), NEG, F32), jnp.zeros((HPG * QB, 1), F32),
                jnp.zeros((HPG * QB, DH), F32))
        carry = lax.fori_loop(0, jnp.maximum(c - 1, 0), lambda kt, cr: sel_tile(cr, kt, b_far), init)
        b_near = bnear_ref[g * HPG:(g + 1) * HPG].reshape(HPG * QB, 2 * QB)
        prev_pen = jnp.where(c > 0, 0.0, NEG)
        carry = sel_tile(carry, jnp.maximum(c - 1, 0), b_near[:, 0:QB] + prev_pen)
        m_s, l_s, acc_s = sel_tile(carry, c, b_near[:, QB:2 * QB])
        o_s = acc_s / jnp.maximum(l_s, 1e-30)

        n_w = WINDOW // QB + 1
        k_parts, v_parts, pens = [], [], []
        for w in range(n_w):
            kt = c - (n_w - 1) + w
            r0 = pl.multiple_of(jnp.maximum(kt, 0) * QB, QB)
            k_parts.append(kw_scr[pl.ds(r0, QB), g * DH:(g + 1) * DH])
            v_parts.append(vw_scr[pl.ds(r0, QB), g * DH:(g + 1) * DH])
            pens.append(jnp.full((1, QB), jnp.where(kt >= 0, 0.0, NEG), F32))
        k_w = jnp.concatenate(k_parts, axis=0)
        v_w = jnp.concatenate(v_parts, axis=0)
        pen_w = jnp.concatenate(pens, axis=1)
        b_w = bwin_ref[g * HPG:(g + 1) * HPG].reshape(HPG * QB, n_w * QB)
        s_w = lax.dot_general(qs, k_w, nt, preferred_element_type=F32) + b_w + pen_w
        p_w = _softmax_rows(s_w)
        o_w = jnp.dot(p_w.astype(BF16), v_w, preferred_element_type=F32)

        for h in range(HPG):
            hh = g * HPG + h
            rows = slice(h * QB, (h + 1) * QB)
            o_h = (gates[:, 3 * hh:3 * hh + 1] * o_c[rows] + gates[:, 3 * hh + 1:3 * hh + 2] * o_s[rows]
                   + gates[:, 3 * hh + 2:3 * hh + 3] * o_w[rows])
            o_ref[:, hh * DH:(hh + 1) * DH] = o_h.astype(o_ref.dtype)


def _nsa_attention(z3, kcmp, vcmp, tabs):
    B, S, _ = z3.shape
    n_qb = S // Q_BLOCK
    n_cp = kcmp.shape[1]
    et, ov, bnear, bfar, bwin, bcmp = tabs

    def zcol(width, col):
        return pl.BlockSpec((None, Q_BLOCK, width), lambda b, c: (b, c, col // width))

    def zfull(col):
        return pl.BlockSpec((None, S, NSA_KV), lambda b, c: (b, 0, col // NSA_KV))

    def const(a):
        nd = a.ndim
        return pl.BlockSpec(a.shape, lambda b, c: (0,) * nd)

    cmp_spec = pl.BlockSpec((None, n_cp, NSA_KV), lambda b, c: (b, 0, 0))
    return pl.pallas_call(
        _nsa_body,
        grid=(B, n_qb),
        in_specs=[zcol(NSA_HEADS * NSA_DH, COL_QN), zcol(LANE, COL_GN),
                  zfull(COL_KC + 2 * NSA_KV), zfull(COL_KC + 3 * NSA_KV),
                  zfull(COL_KC + 4 * NSA_KV), zfull(COL_KC + 5 * NSA_KV),
                  cmp_spec, cmp_spec,
                  const(et), const(ov), const(bnear), const(bfar), const(bwin), const(bcmp)],
        out_specs=pl.BlockSpec((None, Q_BLOCK, BRANCH_W), lambda b, c: (b, c, 0)),
        out_shape=jax.ShapeDtypeStruct((B, S, BRANCH_W), BF16),
        scratch_shapes=[pltpu.VMEM((NSA_GROUPS, S, NSA_DH + et.shape[1]), BF16),
                        pltpu.VMEM((S, NSA_KV), BF16), pltpu.VMEM((S, NSA_KV), BF16),
                        pltpu.VMEM((S, NSA_KV), BF16)],
        compiler_params=_cparams(("parallel", "arbitrary")),
        name="nsa_attention",
    )(z3, z3, z3, z3, z3, z3, kcmp, vcmp, et, ov, bnear, bfar, bwin, bcmp)


def _retention_body(gch_ref, q_ref, k_ref, v_ref, g_ref, cos_ref, sin_ref, decay_ref, hm_ref, xi_ref,
                    zeta_ref, gn_ref, o_ref, state_scr):
    @pl.when(pl.program_id(1) == 0)
    def _():
        state_scr[...] = jnp.zeros_like(state_scr)

    half = RET_HEADS * RET_DK // 2
    cos, sin = cos_ref[...], sin_ref[...]

    def rot(x):
        x1, x2 = x[:, :half], x[:, half:]
        return jnp.concatenate([x1 * cos - x2 * sin, x1 * sin + x2 * cos], axis=1)

    qr = rot(q_ref[...])
    kr = rot(k_ref[...])
    kb = kr.astype(BF16)
    nt = (((1,), (1,)), ((), ()))
    tn = (((0,), (0,)), ((), ()))
    for h in range(RET_HEADS):
        cols = slice(h * RET_DV, (h + 1) * RET_DV)
        vh = v_ref[:, cols].astype(BF16)
        inner = lax.dot_general((qr * hm_ref[h]).astype(BF16), kb, nt, preferred_element_type=F32) * decay_ref[h]
        o = jnp.dot(inner.astype(BF16), vh, preferred_element_type=F32)
        state = state_scr[h]
        o = o + jnp.dot((qr * xi_ref[h]).astype(BF16), state.astype(BF16), preferred_element_type=F32)
        kv = lax.dot_general((kr * zeta_ref[h]).astype(BF16), vh, tn, preferred_element_type=F32)
        state_scr[h] = gch_ref[h] * state + kv
        mu = jnp.mean(o, axis=-1, keepdims=True)
        d = o - mu
        var = jnp.mean(d * d, axis=-1, keepdims=True)
        y = d * lax.rsqrt(var + EPS) * gn_ref[:, cols]
        gate = g_ref[:, cols]
        o_ref[:, cols] = (gate * jax.nn.sigmoid(gate) * y).astype(o_ref.dtype)


def _retention(z3, tabs, gn):
    B, S, _ = z3.shape
    C = RET_CHUNK
    gch, cos, sin, decay, hm, xi, zeta = tabs
    qk_w = RET_HEADS * RET_DK

    def zcol(width, col):
        return pl.BlockSpec((None, C, width), lambda b, n: (b, n, col // width))

    def const(a):
        nd = a.ndim
        return pl.BlockSpec(a.shape, lambda b, n: (0,) * nd)

    pos = pl.BlockSpec((C, qk_w // 2), lambda b, n: (n, 0))
    return pl.pallas_call(
        _retention_body,
        grid=(B, S // C),
        in_specs=[pl.BlockSpec(memory_space=pltpu.SMEM),
                  zcol(qk_w, COL_QR), zcol(qk_w, COL_KR), zcol(BRANCH_W, COL_VR), zcol(BRANCH_W, COL_GR),
                  pos, pos, const(decay), const(hm), const(xi), const(zeta), const(gn)],
        out_specs=pl.BlockSpec((None, C, BRANCH_W), lambda b, n: (b, n, 0)),
        out_shape=jax.ShapeDtypeStruct((B, S, BRANCH_W), BF16),
        scratch_shapes=[pltpu.VMEM((RET_HEADS, qk_w, RET_DV), F32)],
        compiler_params=_cparams(("parallel", "arbitrary")),
        name="retention",
    )(gch, z3, z3, z3, z3, cos, sin, decay, hm, xi, zeta, gn)


CONV_HALO = 32
CONV_ROWS = 64


def _conv_body(a_ref, b_ref, w_ref, cb_ref, lg_ref, lb_ref, o_ref, u_scr):
    tile = a_ref.shape[0]

    @pl.when(pl.program_id(1) == 0)
    def _():
        u_scr[0:CONV_HALO, :] = jnp.zeros((CONV_HALO, u_scr.shape[1]), F32)

    @pl.when(pl.program_id(1) > 0)
    def _():
        u_scr[0:CONV_HALO, :] = u_scr[tile:tile + CONV_HALO, :]

    u_scr[CONV_HALO:CONV_HALO + tile, :] = a_ref[...] * jax.nn.sigmoid(b_ref[...])
    lead = CONV_HALO - (CONV_WIDTH - 1)

    for r0 in range(0, tile, CONV_ROWS):
        acc = jnp.zeros((CONV_ROWS, u_scr.shape[1]), F32) + cb_ref[...]
        for k in range(CONV_WIDTH):
            acc = acc + w_ref[k:k + 1, :] * u_scr[r0 + lead + k:r0 + lead + k + CONV_ROWS, :]
        mu = jnp.mean(acc, axis=-1, keepdims=True)
        d = acc - mu
        var = jnp.mean(d * d, axis=-1, keepdims=True)
        y = d * lax.rsqrt(var + EPS) * lg_ref[...] + lb_ref[...]
        o_ref[r0:r0 + CONV_ROWS, :] = (y * jax.nn.sigmoid(y)).astype(o_ref.dtype)


def _conv_module(z3, w, cb, lg, lb):
    B, S, _ = z3.shape
    tile = min(256, S)

    def zcol(col):
        return pl.BlockSpec((None, tile, BRANCH_W), lambda b, i: (b, i, col // BRANCH_W))

    def const(a):
        return pl.BlockSpec(a.shape, lambda b, i: (0, 0))

    return pl.pallas_call(
        _conv_body,
        grid=(B, S // tile),
        in_specs=[zcol(COL_CA), zcol(COL_CB), const(w), const(cb), const(lg), const(lb)],
        out_specs=pl.BlockSpec((None, tile, BRANCH_W), lambda b, i: (b, i, 0)),
        out_shape=jax.ShapeDtypeStruct((B, S, BRANCH_W), BF16),
        scratch_shapes=[pltpu.VMEM((CONV_HALO + tile, BRANCH_W), F32)],
        compiler_params=_cparams(("parallel", "arbitrary")),
        name="conv_module",
    )(z3, z3, w, cb, lg, lb)


def _merge_body(x_ref, mg_ref, a_ref, r_ref, c_ref, wb_ref, wo_ref, o_ref):
    merged = None
    for i, br in enumerate((a_ref, r_ref, c_ref)):
        proj = jnp.dot(br[...], wb_ref[i], preferred_element_type=F32)
        term = jax.nn.sigmoid(mg_ref[:, i * D_MODEL:(i + 1) * D_MODEL]) * proj
        merged = term if merged is None else merged + term
    o_ref[...] = x_ref[...] + jnp.dot(merged.astype(BF16), wo_ref[...], preferred_element_type=F32)


def _merge(x2, z2, o_nsa, o_ret, o_conv, wb, wo):
    T = x2.shape[0]
    tm = 512
    br = pl.BlockSpec((tm, BRANCH_W), lambda i: (i, 0))
    return pl.pallas_call(
        _merge_body,
        grid=(T // tm,),
        in_specs=[pl.BlockSpec((tm, D_MODEL), lambda i: (i, 0)),
                  pl.BlockSpec((tm, N_BRANCH * D_MODEL), lambda i: (i, COL_MG // (N_BRANCH * D_MODEL))),
                  br, br, br,
                  pl.BlockSpec(wb.shape, lambda i: (0, 0, 0)),
                  pl.BlockSpec(wo.shape, lambda i: (0, 0))],
        out_specs=pl.BlockSpec((tm, D_MODEL), lambda i: (i, 0)),
        out_shape=jax.ShapeDtypeStruct((T, D_MODEL), F32),
        compiler_params=_cparams(("parallel",)),
        name="merge_out_proj",
    )(x2, z2, o_nsa, o_ret, o_conv, wb, wo)


FF_CHUNK = 1024


def _ffn_body(x_ref, g_ref, w1_ref, w2_ref, gf_ref, o_ref, *, final_norm):
    x = x_ref[...]
    ms = jnp.mean(x * x, axis=-1, keepdims=True)
    h = (x * lax.rsqrt(ms + EPS) * g_ref[...]).astype(BF16)
    acc = x
    for j in range(D_FF // FF_CHUNK):
        cols = slice(j * FF_CHUNK, (j + 1) * FF_CHUNK)
        a = jnp.maximum(jnp.dot(h, w1_ref[:, cols], preferred_element_type=F32), 0.0)
        acc = acc + jnp.dot((a * a).astype(BF16), w2_ref[cols, :], preferred_element_type=F32)
    if final_norm:
        ms = jnp.mean(acc * acc, axis=-1, keepdims=True)
        acc = acc * lax.rsqrt(ms + EPS) * gf_ref[...]
    o_ref[...] = acc


def _ffn(x2, g, w1, w2, gf, final_norm):
    T = x2.shape[0]
    tm = 512
    vec = pl.BlockSpec((1, D_MODEL), lambda i: (0, 0))
    return pl.pallas_call(
        functools.partial(_ffn_body, final_norm=final_norm),
        grid=(T // tm,),
        in_specs=[pl.BlockSpec((tm, D_MODEL), lambda i: (i, 0)), vec,
                  pl.BlockSpec(w1.shape, lambda i: (0, 0)),
                  pl.BlockSpec(w2.shape, lambda i: (0, 0)), vec],
        out_specs=pl.BlockSpec((tm, D_MODEL), lambda i: (i, 0)),
        out_shape=jax.ShapeDtypeStruct((T, D_MODEL), F32),
        compiler_params=_cparams(("parallel",)),
        name="ffn",
    )(x2, g, w1, w2, gf)


def _rel_bucket(dist):
    n = jnp.maximum(dist, 0)
    max_exact = REL_BUCKETS // 2
    nf = jnp.maximum(n, 1).astype(F32)
    large = max_exact + (jnp.log(nf / max_exact) / math.log(REL_MAX_DIST / max_exact)
                         * (REL_BUCKETS - max_exact)).astype(jnp.int32)
    large = jnp.minimum(large, REL_BUCKETS - 1)
    return jnp.where(n < max_exact, n, large)


def _nsa_tables(rel_table, S):
    n_s = S // SEL_BLOCK
    n_cp = S // CMP_STRIDE
    QB = Q_BLOCK

    def bias(dist):
        return jnp.transpose(rel_table[_rel_bucket(dist)], (2, 0, 1)).astype(F32)

    i = jnp.arange(QB)[:, None]
    et = (jnp.arange(S)[:, None] // SEL_BLOCK == jnp.arange(n_s)[None, :]).astype(BF16)
    cmp_start = jnp.arange(n_cp) * CMP_STRIDE
    sel_start = jnp.arange(n_s) * SEL_BLOCK
    ov = ((cmp_start[:, None] <= sel_start[None, :] + SEL_BLOCK - 1)
          & (cmp_start[:, None] + CMP_BLOCK - 1 >= sel_start[None, :])
          & (jnp.arange(n_cp)[:, None] < n_cp - 1)).astype(BF16)
    d_near = i + QB - jnp.arange(2 * QB)[None, :]
    bnear = jnp.where(d_near >= 0, bias(d_near), NEG)
    bfar = bias(jnp.full((QB, QB), S + REL_MAX_DIST, jnp.int32))
    n_wk = WINDOW + QB
    d_win = i + WINDOW - jnp.arange(n_wk)[None, :]
    bwin = jnp.where((d_win >= 0) & (d_win < WINDOW), bias(d_win), NEG)
    d_cmp = i - CMP_STRIDE * (jnp.arange(n_cp)[None, :] - n_cp // 2) - (CMP_BLOCK - 1)
    bcmp = bias(jnp.where(d_cmp >= 0, d_cmp, S + REL_MAX_DIST))
    return et, ov, bnear, bfar, bwin, bcmp


def _retention_tables(S):
    H, DK, C = RET_HEADS, RET_DK, RET_CHUNK
    half = DK // 2
    pos = jnp.arange(S, dtype=F32)
    inv = ROPE_BASE ** (-jnp.arange(half, dtype=F32) / half)
    ang = pos[:, None] * inv[None, :]
    cos = jnp.tile(jnp.cos(ang), (1, H))
    sin = jnp.tile(jnp.sin(ang), (1, H))
    log_g = jnp.log1p(-jnp.exp2(-5.0 - jnp.arange(H, dtype=F32)))
    ar = jnp.arange(C)
    diff = ar[:, None] - ar[None, :]
    decay = jnp.where(diff >= 0, jnp.exp(log_g[:, None, None] * jnp.maximum(diff, 0).astype(F32)), 0.0)
    zeta = jnp.exp(log_g[:, None] * (C - 1 - ar).astype(F32))
    xi = jnp.exp(log_g[:, None] * (ar + 1).astype(F32))
    gch = jnp.exp(log_g * C)
    lane_head = (jnp.arange(H * DK) % (H * half)) // half
    hm = (lane_head[None, :] == jnp.arange(H)[:, None]).astype(F32)
    return (gch, cos, sin, decay, hm[:, None, :], xi[:, :, None] * hm[:, None, :],
            zeta[:, :, None] * hm[:, None, :])


def _pack_w_in(w):
    o = np.cumsum([0, 512, 128, 128, 128, 128, 128, 128, 24, 256, 256, 512, 512, 512, 512, 3072])
    seg = lambda i: w[:, int(o[i]):int(o[i + 1])]

    def halves_major(a):
        a4 = a.reshape(a.shape[0], RET_HEADS, 2, RET_DK // 2)
        return jnp.transpose(a4, (0, 2, 1, 3)).reshape(a.shape[0], RET_HEADS * RET_DK)

    parts = [seg(14), seg(0) * (NSA_DH ** -0.5), seg(10), seg(11), seg(12), seg(13),
             halves_major(seg(8)), halves_major(seg(9)) * (RET_DK ** -0.5),
             seg(1), seg(2), seg(3), seg(4), seg(5), seg(6),
             seg(7), jnp.zeros((w.shape[0], LANE - 3 * NSA_HEADS), w.dtype)]
    return jnp.concatenate(parts, axis=1).astype(BF16)


def _pack_compress(pe, w1, w2):
    G, DH, r = NSA_GROUPS, NSA_DH, CMP_BLOCK // CMP_STRIDE
    eye = jnp.eye(G, dtype=w1.dtype)
    pe2 = jnp.tile(pe.reshape(r, CMP_STRIDE, 1, DH), (1, 1, G, 1)).reshape(r, CMP_STRIDE * G * DH)
    w1r = w1.reshape(r, CMP_STRIDE, DH, CMP_HIDDEN)
    w1b = jnp.einsum('rldf,gk->rlgdkf', w1r, eye).reshape(r, CMP_STRIDE * G * DH, G * CMP_HIDDEN)
    w2b = jnp.einsum('fd,gk->gfkd', w2, eye).reshape(G * CMP_HIDDEN, G * DH)
    return pe2, w1b.astype(BF16), w2b.astype(BF16)


def kernel(x, rel_table, norm_mix, w_in, cmp_pe_k, cmp_w1_k, cmp_w2_k, cmp_pe_v, cmp_w1_v, cmp_w2_v, ret_gn, conv_w, conv_b, conv_ln_g, conv_ln_b, w_branch, w_out, norm_mlp, w_ff1, w_ff2, norm_final):
    B, S, D = x.shape
    depth = w_in.shape[0]
    T = B * S
    nsa_tabs = _nsa_tables(rel_table, S)
    ret_tabs = _retention_tables(S)
    x2 = x.reshape(T, D)
    for l in range(depth):
        z2 = _in_proj(x2, norm_mix[l][None, :], _pack_w_in(w_in[l]))
        z3 = z2.reshape(B, S, Z_COLS)
        n_ch = S // CMP_STRIDE
        kc2 = z3[:, :, COL_KC:COL_KC + NSA_KV].reshape(B, n_ch, CMP_STRIDE * NSA_KV)
        vc2 = z3[:, :, COL_KC + NSA_KV:COL_KC + 2 * NSA_KV].reshape(B, n_ch, CMP_STRIDE * NSA_KV)
        kcmp, vcmp = _compress(kc2, vc2, *_pack_compress(cmp_pe_k[l], cmp_w1_k[l], cmp_w2_k[l]),
                               *_pack_compress(cmp_pe_v[l], cmp_w1_v[l], cmp_w2_v[l]))
        o_nsa = _nsa_attention(z3, kcmp, vcmp, nsa_tabs)
        o_ret = _retention(z3, ret_tabs, ret_gn[l][None, :])
        conv_w_pad = jnp.concatenate([conv_w[l], jnp.zeros((1, BRANCH_W), conv_w.dtype)], axis=0)
        o_conv = _conv_module(z3, conv_w_pad, conv_b[l][None, :], conv_ln_g[l][None, :], conv_ln_b[l][None, :])
        x2 = _merge(x2, z2, o_nsa.reshape(T, BRANCH_W), o_ret.reshape(T, BRANCH_W), o_conv.reshape(T, BRANCH_W),
                    w_branch[l].astype(BF16), w_out[l].astype(BF16))
        x2 = _ffn(x2, norm_mlp[l][None, :], w_ff1[l].astype(BF16), w_ff2[l].astype(BF16), norm_final[None, :],
                  final_norm=(l == depth - 1))
    return x2.reshape(B, S, D)
```

```python
import functools
import math

import jax
import jax.numpy as jnp
import numpy as np
from jax import lax
from jax.experimental import pallas as pl
from jax.experimental.pallas import tpu as pltpu

F32 = jnp.float32
BF16 = jnp.bfloat16

D_MODEL = 1024
BRANCH_W = 512
N_BRANCH = 3
NSA_HEADS = 8
NSA_GROUPS = 2
NSA_HPG = 4
NSA_DH = 64
NSA_KV = NSA_GROUPS * NSA_DH
CMP_BLOCK = 32
CMP_STRIDE = 16
CMP_HIDDEN = 128
SEL_BLOCK = 64
SEL_TOPK = 8
WINDOW = 512
Q_BLOCK = 128
RET_HEADS = 4
RET_DV = 128
RET_DK = 64
RET_CHUNK = 128
ROPE_BASE = 10000.0
CONV_WIDTH = 31
REL_BUCKETS = 32
REL_MAX_DIST = 128
D_FF = 4096
EPS = 1e-6
NEG = -1e30

LANE = 128
VMEM_LIMIT = 56 * 1024 * 1024

COL_MG = 0
COL_QN = 3072
COL_VR = 3584
COL_GR = 4096
COL_CA = 4608
COL_CB = 5120
COL_QR = 5632
COL_KR = 5888
COL_KC = 6144
COL_GN = 6912
Z_COLS = 7040


def _cparams(sem):
    return pltpu.CompilerParams(dimension_semantics=sem, vmem_limit_bytes=VMEM_LIMIT)


def _in_proj_body(x_ref, g_ref, w_ref, z_ref, h_scr):
    @pl.when(pl.program_id(1) == 0)
    def _():
        x = x_ref[...]
        ms = jnp.mean(x * x, axis=-1, keepdims=True)
        h_scr[...] = (x * lax.rsqrt(ms + EPS) * g_ref[...]).astype(BF16)

    z_ref[...] = jnp.dot(h_scr[...], w_ref[...], preferred_element_type=F32)


def _in_proj(x2, g, w):
    T = x2.shape[0]
    tm, tn = 512, 1408
    return pl.pallas_call(
        _in_proj_body,
        grid=(T // tm, Z_COLS // tn),
        in_specs=[pl.BlockSpec((tm, D_MODEL), lambda i, j: (i, 0)),
                  pl.BlockSpec((1, D_MODEL), lambda i, j: (0, 0)),
                  pl.BlockSpec((D_MODEL, tn), lambda i, j: (0, j))],
        out_specs=pl.BlockSpec((tm, tn), lambda i, j: (i, j)),
        out_shape=jax.ShapeDtypeStruct((T, Z_COLS), F32),
        scratch_shapes=[pltpu.VMEM((tm, D_MODEL), BF16)],
        compiler_params=_cparams(("parallel", "arbitrary")),
        name="in_proj",
    )(x2, g, w)


def _compress_body(kc_ref, vc_ref, pek_ref, w1k_ref, w2k_ref, pev_ref, w1v_ref, w2v_ref, ko_ref, vo_ref):
    def one(c_ref, pe_ref, w1_ref, w2_ref, o_ref):
        ch = c_ref[...]
        a = jnp.dot((ch + pe_ref[0:1, :]).astype(BF16), w1_ref[0], preferred_element_type=F32)
        b = jnp.dot((ch + pe_ref[1:2, :]).astype(BF16), w1_ref[1], preferred_element_type=F32)
        n = b.shape[0]
        hid = jax.nn.gelu(a + pltpu.roll(b, n - 1, 0))
        o_ref[...] = jnp.dot(hid.astype(BF16), w2_ref[...], preferred_element_type=F32).astype(o_ref.dtype)

    one(kc_ref, pek_ref, w1k_ref, w2k_ref, ko_ref)
    one(vc_ref, pev_ref, w1v_ref, w2v_ref, vo_ref)


def _compress(kc2, vc2, pek, w1k, w2k, pev, w1v, w2v):
    B, n_ch, wide = kc2.shape
    hid2 = NSA_GROUPS * CMP_HIDDEN
    act = pl.BlockSpec((None, n_ch, wide), lambda b: (b, 0, 0))
    pe = pl.BlockSpec((2, wide), lambda b: (0, 0))
    w1 = pl.BlockSpec((2, wide, hid2), lambda b: (0, 0, 0))
    w2 = pl.BlockSpec((hid2, NSA_KV), lambda b: (0, 0))
    out = pl.BlockSpec((None, n_ch, NSA_KV), lambda b: (b, 0, 0))
    return pl.pallas_call(
        _compress_body,
        grid=(B,),
        in_specs=[act, act, pe, w1, w2, pe, w1, w2],
        out_specs=[out, out],
        out_shape=[jax.ShapeDtypeStruct((B, n_ch, NSA_KV), BF16)] * 2,
        compiler_params=_cparams(("parallel",)),
        name="nsa_compress",
    )(kc2, vc2, pek, w1k, w2k, pev, w1v, w2v)


SEL_TILE = 2 * Q_BLOCK
NEAR_TILES = 3


def _nsa_body(q_ref, gn_ref, ks_ref, kw_ref, vst_ref, vwt_ref, kcmp_ref, vcmpt_ref,
              et_ref, ovt_ref, bnear_ref, cfar_ref, bwin_ref, bcmp_ref, o_ref, kaug_scr, kw_scr, part_scr, sa_scr, sb_scr, pa_scr, pb_scr):
    c = pl.program_id(1)
    n_s = et_ref.shape[1]
    n_cp = kcmp_ref.shape[1]
    QB, DH, HPG = Q_BLOCK, NSA_DH, NSA_HPG
    s_len = ks_ref.shape[0]

    @pl.when(c == 0)
    def _():
        for g in range(NSA_GROUPS):
            kaug_scr[g, 0:s_len, 0:DH] = ks_ref[:, g * DH:(g + 1) * DH].astype(BF16)
            kaug_scr[g, 0:s_len, DH:DH + n_s] = et_ref[...]
            kaug_scr[g, s_len:s_len + SEL_TILE, 0:DH] = jnp.zeros((SEL_TILE, DH), BF16)
            kaug_scr[g, s_len:s_len + SEL_TILE, DH:DH + n_s] = jnp.ones((SEL_TILE, n_s), BF16)
            kw_scr[g] = kw_ref[:, g * DH:(g + 1) * DH].astype(BF16)

    gates = jax.nn.sigmoid(gn_ref[...])

    blk = lax.broadcasted_iota(jnp.int32, (n_s, QB), 0)
    cur = (c * QB + lax.broadcasted_iota(jnp.int32, (n_s, QB), 1)) // SEL_BLOCK
    valid = blk <= cur
    forced = (blk == 0) | (blk == cur) | (blk == cur - 1)
    blk_f = blk.astype(F32)

    cmp_off = pl.multiple_of(n_cp - c * (QB // CMP_STRIDE), QB // CMP_STRIDE)
    n_far = jnp.maximum(c - 1, 0)
    n_w = WINDOW // QB + 1

    def online(carry, s, vt):
        m_i, l_i, acc = carry
        m_n = jnp.maximum(m_i, jnp.max(s, axis=0, keepdims=True))
        alpha = jnp.exp(m_i - m_n)
        p = jnp.exp(s - m_n)
        l_n = alpha * l_i + jnp.sum(p, axis=0, keepdims=True)
        acc_n = alpha * acc + jnp.dot(vt, p.astype(BF16), preferred_element_type=F32)
        return m_n, l_n, acc_n

    def gate_row(g, branch):
        return jnp.concatenate([gates[3 * (g * HPG + h) + branch:3 * (g * HPG + h) + branch + 1, :]
                                for h in range(HPG)], axis=1)

    hds = [slice(g * DH, (g + 1) * DH) for g in range(NSA_GROUPS)]
    qas = []
    for g in range(NSA_GROUPS):
        hd = hds[g]
        qt = jnp.concatenate([q_ref[(g * HPG + h) * DH:(g * HPG + h + 1) * DH, :] for h in range(HPG)], axis=1)

        s_c = (jnp.dot(kcmp_ref[g], qt, preferred_element_type=F32)
               + bcmp_ref[g, pl.ds(cmp_off, n_cp), :])
        e_c = jnp.where(s_c > 0.5 * NEG, jnp.exp(s_c - jnp.max(s_c, axis=0, keepdims=True)), 0.0)
        p_c = e_c * (1.0 / jnp.maximum(jnp.sum(e_c, axis=0, keepdims=True), 1e-30))
        o_c = jnp.dot(vcmpt_ref[g], p_c.astype(BF16), preferred_element_type=F32)

        p_sum = p_c[:, 0:QB] + p_c[:, QB:2 * QB] + p_c[:, 2 * QB:3 * QB] + p_c[:, 3 * QB:4 * QB]
        p_hi = p_sum.astype(BF16)
        p_lo = (p_sum - p_hi.astype(F32)).astype(BF16)
        imp = (jnp.dot(ovt_ref[...], p_hi, preferred_element_type=F32)
               + jnp.dot(ovt_ref[...], p_lo, preferred_element_type=F32))
        score = jnp.where(forced, imp + 1e4, jnp.where(valid, imp, -1e4))
        selneg = jnp.full((n_s, QB), NEG, F32)
        for _ in range(min(SEL_TOPK, n_s)):
            m = jnp.max(score, axis=0, keepdims=True)
            first = jnp.min(jnp.where(score == m, blk_f, float(n_s)), axis=0, keepdims=True)
            pick = blk_f == first
            selneg = jnp.where(pick, 0.0, selneg)
            score = jnp.where(pick, -jnp.inf, score)
        selneg = selneg.astype(BF16)

        qas.append(jnp.concatenate([qt, jnp.concatenate([selneg] * HPG, axis=1)], axis=0))

        w_tiles = [c - (n_w - 1) + w for w in range(n_w)]
        w_starts = [pl.multiple_of(jnp.maximum(t, 0) * QB, QB) for t in w_tiles]
        k_w = jnp.concatenate([kw_scr[g, pl.ds(r0, QB), :] for r0 in w_starts], axis=0)
        v_w = jnp.concatenate([vwt_ref[hd, pl.ds(r0, QB)] for r0 in w_starts], axis=1)
        s_w = jnp.dot(k_w, qt, preferred_element_type=F32) + bwin_ref[g]
        s_w = jnp.concatenate([s_w[w * QB:(w + 1) * QB] + jnp.where(w_tiles[w] >= 0, 0.0, NEG)
                               for w in range(n_w - 1)] + [s_w[(n_w - 1) * QB:]], axis=0)
        e_w = jnp.exp(s_w - jnp.max(s_w, axis=0, keepdims=True))
        o_w = (jnp.dot(v_w, e_w.astype(BF16), preferred_element_type=F32)
               * (1.0 / jnp.maximum(jnp.sum(e_w, axis=0, keepdims=True), 1e-30)))

        part_scr[g] = gate_row(g, 0) * o_c + gate_row(g, 2) * o_w

    n_it = n_far // 2
    groups = range(NSA_GROUPS)

    def scores(kt, g, dst):
        r0 = pl.multiple_of(jnp.where(kt < n_it, kt * SEL_TILE, s_len), SEL_TILE)
        dst[g] = jnp.dot(kaug_scr[g, pl.ds(r0, SEL_TILE), :], qas[g], preferred_element_type=F32)

    def values(kt, g, p_src):
        r0 = pl.multiple_of(jnp.clip(kt, 0, jnp.maximum(n_it - 1, 0)) * SEL_TILE, SEL_TILE)
        return jnp.dot(vst_ref[hds[g], pl.ds(r0, SEL_TILE)], p_src[g], preferred_element_type=F32)

    def half(kt, carries, src, dst, p_src, p_dst):
        for g in groups:
            scores(kt + 1, g, dst)
        pv = [values(kt - 1, g, p_src) for g in groups]
        out = []
        for g in groups:
            m_i, l_i, acc, alpha_prev = carries[g]
            s = src[g]
            m_n = jnp.maximum(m_i, jnp.max(s, axis=0, keepdims=True))
            alpha = jnp.exp(m_i - m_n)
            p = jnp.exp(s - m_n)
            p_dst[g] = p.astype(BF16)
            out.append((m_n, alpha * l_i + jnp.sum(p, axis=0, keepdims=True), alpha_prev * acc + pv[g], alpha))
        return tuple(out)

    def far(j, carries):
        first = half(2 * j, carries, sa_scr, sb_scr, pb_scr, pa_scr)
        return half(2 * j + 1, first, sb_scr, sa_scr, pa_scr, pb_scr)

    for g in groups:
        pb_scr[g] = jnp.zeros((SEL_TILE, HPG * QB), BF16)
        scores(0, g, sa_scr)
    init = (jnp.full((1, HPG * QB), NEG, F32), jnp.zeros((1, HPG * QB), F32),
            jnp.zeros((DH, HPG * QB), F32), jnp.ones((1, HPG * QB), F32))
    n_trips = (n_it + 1) // 2
    piped = lax.fori_loop(0, n_trips, far, (init,) * NSA_GROUPS)
    carries = [(m_i, l_i, alpha_prev * acc + values(2 * n_trips - 1, g, pb_scr))
               for g, (m_i, l_i, acc, alpha_prev) in enumerate(piped)]

    tiles = (jnp.maximum(n_far - 1, 0), jnp.maximum(c - 1, 0), c)
    pens = (jnp.where(n_far % 2 == 1, 0.0, NEG), jnp.where(c >= 1, 0.0, NEG), None)
    starts = [pl.multiple_of(t * QB, QB) for t in tiles]
    for g in range(NSA_GROUPS):
        m_f, l_f, acc_f = carries[g]
        carry = (m_f + cfar_ref[g], l_f, acc_f)
        k_n = jnp.concatenate([kaug_scr[g, pl.ds(r0, QB), :] for r0 in starts], axis=0)
        v_n = jnp.concatenate([vst_ref[hds[g], pl.ds(r0, QB)] for r0 in starts], axis=1)
        s_n = jnp.dot(k_n, qas[g], preferred_element_type=F32) + bnear_ref[g]
        s_n = jnp.concatenate([s_n[i * QB:(i + 1) * QB] if pen is None else s_n[i * QB:(i + 1) * QB] + pen
                               for i, pen in enumerate(pens)], axis=0)
        m_s, l_s, acc_s = online(carry, s_n, v_n)
        o_g = part_scr[g] + acc_s * (gate_row(g, 1) * (1.0 / jnp.maximum(l_s, 1e-30)))
        for h in range(HPG):
            hh = g * HPG + h
            o_ref[hh * DH:(hh + 1) * DH, :] = o_g[:, h * QB:(h + 1) * QB].astype(o_ref.dtype)


def _nsa_attention(z3, qt, gnt, vst, vwt, kcmp, vcmpt, tabs):
    B, S, _ = z3.shape
    et, ovt, bnear, cfar, bwin, bcmp = tabs

    def tcol(rows):
        return pl.BlockSpec((None, rows, Q_BLOCK), lambda b, c: (b, 0, c))

    def zfull(col):
        return pl.BlockSpec((None, S, NSA_KV), lambda b, c: (b, 0, col // NSA_KV))

    def per_batch(a):
        nd = a.ndim - 1
        return pl.BlockSpec((None,) + a.shape[1:], lambda b, c: (b,) + (0,) * nd)

    def const(a):
        nd = a.ndim
        return pl.BlockSpec(a.shape, lambda b, c: (0,) * nd)

    return pl.pallas_call(
        _nsa_body,
        grid=(B, S // Q_BLOCK),
        in_specs=[tcol(BRANCH_W), tcol(gnt.shape[1]),
                  zfull(COL_KC + 2 * NSA_KV), zfull(COL_KC + 4 * NSA_KV),
                  per_batch(vst), per_batch(vwt), per_batch(kcmp), per_batch(vcmpt),
                  const(et), const(ovt), const(bnear), const(cfar), const(bwin), const(bcmp)],
        out_specs=tcol(BRANCH_W),
        out_shape=jax.ShapeDtypeStruct((B, BRANCH_W, S), BF16),
        scratch_shapes=[pltpu.VMEM((NSA_GROUPS, S + SEL_TILE, NSA_DH + et.shape[1]), BF16),
                        pltpu.VMEM((NSA_GROUPS, S, NSA_DH), BF16),
                        pltpu.VMEM((NSA_GROUPS, NSA_DH, NSA_HPG * Q_BLOCK), F32),
                        pltpu.VMEM((NSA_GROUPS, SEL_TILE, NSA_HPG * Q_BLOCK), F32),
                        pltpu.VMEM((NSA_GROUPS, SEL_TILE, NSA_HPG * Q_BLOCK), F32),
                        pltpu.VMEM((NSA_GROUPS, SEL_TILE, NSA_HPG * Q_BLOCK), BF16),
                        pltpu.VMEM((NSA_GROUPS, SEL_TILE, NSA_HPG * Q_BLOCK), BF16)],
        compiler_params=_cparams(("parallel", "arbitrary")),
        name="nsa_attention",
    )(qt, gnt, z3, z3, vst, vwt, kcmp, vcmpt, et, ovt, bnear, cfar, bwin, bcmp)


def _retention_body(gch_ref, q_ref, k_ref, v_ref, g_ref, cos_ref, sin_ref, decay_ref, hm_ref, xi_ref,
                    zeta_ref, gn_ref, o_ref, state_scr):
    @pl.when(pl.program_id(1) == 0)
    def _():
        state_scr[...] = jnp.zeros_like(state_scr)

    half = RET_HEADS * RET_DK // 2
    cos, sin = cos_ref[...], sin_ref[...]

    def rot(x):
        x1, x2 = x[:, :half], x[:, half:]
        return jnp.concatenate([x1 * cos - x2 * sin, x1 * sin + x2 * cos], axis=1)

    qr = rot(q_ref[...])
    kr = rot(k_ref[...])
    kb = kr.astype(BF16)
    nt = (((1,), (1,)), ((), ()))
    tn = (((0,), (0,)), ((), ()))
    for h in range(RET_HEADS):
        cols = slice(h * RET_DV, (h + 1) * RET_DV)
        vh = v_ref[:, cols].astype(BF16)
        inner = lax.dot_general((qr * hm_ref[h]).astype(BF16), kb, nt, preferred_element_type=F32) * decay_ref[h]
        o = jnp.dot(inner.astype(BF16), vh, preferred_element_type=F32)
        state = state_scr[h]
        o = o + jnp.dot((qr * xi_ref[h]).astype(BF16), state.astype(BF16), preferred_element_type=F32)
        kv = lax.dot_general((kr * zeta_ref[h]).astype(BF16), vh, tn, preferred_element_type=F32)
        state_scr[h] = gch_ref[h] * state + kv
        mu = jnp.mean(o, axis=-1, keepdims=True)
        d = o - mu
        var = jnp.mean(d * d, axis=-1, keepdims=True)
        y = d * lax.rsqrt(var + EPS) * gn_ref[:, cols]
        gate = g_ref[:, cols]
        o_ref[:, cols] = (gate * jax.nn.sigmoid(gate) * y).astype(o_ref.dtype)


def _retention(z3, tabs, gn):
    B, S, _ = z3.shape
    C = RET_CHUNK
    gch, cos, sin, decay, hm, xi, zeta = tabs
    qk_w = RET_HEADS * RET_DK

    def zcol(width, col):
        return pl.BlockSpec((None, C, width), lambda b, n: (b, n, col // width))

    def const(a):
        nd = a.ndim
        return pl.BlockSpec(a.shape, lambda b, n: (0,) * nd)

    pos = pl.BlockSpec((C, qk_w // 2), lambda b, n: (n, 0))
    return pl.pallas_call(
        _retention_body,
        grid=(B, S // C),
        in_specs=[pl.BlockSpec(memory_space=pltpu.SMEM),
                  zcol(qk_w, COL_QR), zcol(qk_w, COL_KR), zcol(BRANCH_W, COL_VR), zcol(BRANCH_W, COL_GR),
                  pos, pos, const(decay), const(hm), const(xi), const(zeta), const(gn)],
        out_specs=pl.BlockSpec((None, C, BRANCH_W), lambda b, n: (b, n, 0)),
        out_shape=jax.ShapeDtypeStruct((B, S, BRANCH_W), BF16),
        scratch_shapes=[pltpu.VMEM((RET_HEADS, qk_w, RET_DV), F32)],
        compiler_params=_cparams(("parallel", "arbitrary")),
        name="retention",
    )(gch, z3, z3, z3, z3, cos, sin, decay, hm, xi, zeta, gn)


CONV_HALO = 32
CONV_ROWS = 64


def _conv_body(a_ref, b_ref, w_ref, cb_ref, lg_ref, lb_ref, o_ref, u_scr):
    tile = a_ref.shape[0]

    @pl.when(pl.program_id(1) == 0)
    def _():
        u_scr[0:CONV_HALO, :] = jnp.zeros((CONV_HALO, u_scr.shape[1]), F32)

    @pl.when(pl.program_id(1) > 0)
    def _():
        u_scr[0:CONV_HALO, :] = u_scr[tile:tile + CONV_HALO, :]

    u_scr[CONV_HALO:CONV_HALO + tile, :] = a_ref[...] * jax.nn.sigmoid(b_ref[...])
    lead = CONV_HALO - (CONV_WIDTH - 1)

    for r0 in range(0, tile, CONV_ROWS):
        acc = jnp.zeros((CONV_ROWS, u_scr.shape[1]), F32) + cb_ref[...]
        for k in range(CONV_WIDTH):
            acc = acc + w_ref[k:k + 1, :] * u_scr[r0 + lead + k:r0 + lead + k + CONV_ROWS, :]
        mu = jnp.mean(acc, axis=-1, keepdims=True)
        d = acc - mu
        var = jnp.mean(d * d, axis=-1, keepdims=True)
        y = d * lax.rsqrt(var + EPS) * lg_ref[...] + lb_ref[...]
        o_ref[r0:r0 + CONV_ROWS, :] = (y * jax.nn.sigmoid(y)).astype(o_ref.dtype)


def _conv_module(z3, w, cb, lg, lb):
    B, S, _ = z3.shape
    tile = min(256, S)

    def zcol(col):
        return pl.BlockSpec((None, tile, BRANCH_W), lambda b, i: (b, i, col // BRANCH_W))

    def const(a):
        return pl.BlockSpec(a.shape, lambda b, i: (0, 0))

    return pl.pallas_call(
        _conv_body,
        grid=(B, S // tile),
        in_specs=[zcol(COL_CA), zcol(COL_CB), const(w), const(cb), const(lg), const(lb)],
        out_specs=pl.BlockSpec((None, tile, BRANCH_W), lambda b, i: (b, i, 0)),
        out_shape=jax.ShapeDtypeStruct((B, S, BRANCH_W), BF16),
        scratch_shapes=[pltpu.VMEM((CONV_HALO + tile, BRANCH_W), F32)],
        compiler_params=_cparams(("parallel", "arbitrary")),
        name="conv_module",
    )(z3, z3, w, cb, lg, lb)


def _merge_body(x_ref, mg_ref, a_ref, r_ref, c_ref, wb_ref, wo_ref, o_ref):
    merged = None
    for i, br in enumerate((a_ref, r_ref, c_ref)):
        proj = jnp.dot(br[...], wb_ref[i], preferred_element_type=F32)
        term = jax.nn.sigmoid(mg_ref[:, i * D_MODEL:(i + 1) * D_MODEL]) * proj
        merged = term if merged is None else merged + term
    o_ref[...] = x_ref[...] + jnp.dot(merged.astype(BF16), wo_ref[...], preferred_element_type=F32)


def _merge(x2, z2, o_nsa, o_ret, o_conv, wb, wo):
    T = x2.shape[0]
    tm = 512
    br = pl.BlockSpec((tm, BRANCH_W), lambda i: (i, 0))
    return pl.pallas_call(
        _merge_body,
        grid=(T // tm,),
        in_specs=[pl.BlockSpec((tm, D_MODEL), lambda i: (i, 0)),
                  pl.BlockSpec((tm, N_BRANCH * D_MODEL), lambda i: (i, COL_MG // (N_BRANCH * D_MODEL))),
                  br, br, br,
                  pl.BlockSpec(wb.shape, lambda i: (0, 0, 0)),
                  pl.BlockSpec(wo.shape, lambda i: (0, 0))],
        out_specs=pl.BlockSpec((tm, D_MODEL), lambda i: (i, 0)),
        out_shape=jax.ShapeDtypeStruct((T, D_MODEL), F32),
        compiler_params=_cparams(("parallel",)),
        name="merge_out_proj",
    )(x2, z2, o_nsa, o_ret, o_conv, wb, wo)


FF_CHUNK = 1024


def _ffn_body(x_ref, g_ref, w1_ref, w2_ref, gf_ref, o_ref, *, final_norm):
    x = x_ref[...]
    ms = jnp.mean(x * x, axis=-1, keepdims=True)
    h = (x * lax.rsqrt(ms + EPS) * g_ref[...]).astype(BF16)
    acc = x
    for j in range(D_FF // FF_CHUNK):
        cols = slice(j * FF_CHUNK, (j + 1) * FF_CHUNK)
        a = jnp.maximum(jnp.dot(h, w1_ref[:, cols], preferred_element_type=F32), 0.0)
        acc = acc + jnp.dot((a * a).astype(BF16), w2_ref[cols, :], preferred_element_type=F32)
    if final_norm:
        ms = jnp.mean(acc * acc, axis=-1, keepdims=True)
        acc = acc * lax.rsqrt(ms + EPS) * gf_ref[...]
    o_ref[...] = acc


def _ffn(x2, g, w1, w2, gf, final_norm):
    T = x2.shape[0]
    tm = 512
    vec = pl.BlockSpec((1, D_MODEL), lambda i: (0, 0))
    return pl.pallas_call(
        functools.partial(_ffn_body, final_norm=final_norm),
        grid=(T // tm,),
        in_specs=[pl.BlockSpec((tm, D_MODEL), lambda i: (i, 0)), vec,
                  pl.BlockSpec(w1.shape, lambda i: (0, 0)),
                  pl.BlockSpec(w2.shape, lambda i: (0, 0)), vec],
        out_specs=pl.BlockSpec((tm, D_MODEL), lambda i: (i, 0)),
        out_shape=jax.ShapeDtypeStruct((T, D_MODEL), F32),
        compiler_params=_cparams(("parallel",)),
        name="ffn",
    )(x2, g, w1, w2, gf)


def _rel_bucket(dist):
    n = jnp.maximum(dist, 0)
    max_exact = REL_BUCKETS // 2
    nf = jnp.maximum(n, 1).astype(F32)
    large = max_exact + (jnp.log(nf / max_exact) / math.log(REL_MAX_DIST / max_exact)
                         * (REL_BUCKETS - max_exact)).astype(jnp.int32)
    large = jnp.minimum(large, REL_BUCKETS - 1)
    return jnp.where(n < max_exact, n, large)


def _nsa_tables(rel_table, S):
    n_s = S // SEL_BLOCK
    n_cp = S // CMP_STRIDE
    QB, G, HPG = Q_BLOCK, NSA_GROUPS, NSA_HPG

    def bias(dist):
        onehot = _rel_bucket(dist)[None, :, :, None] == jnp.arange(REL_BUCKETS)
        full = jnp.sum(jnp.where(onehot, rel_table.T.astype(F32)[:, None, None, :], 0.0), axis=-1)
        full = full.reshape(G, HPG, dist.shape[0], QB)
        return jnp.transpose(full, (0, 2, 1, 3)).reshape(G, dist.shape[0], HPG * QB)

    def masked(dist, ok):
        return jnp.where(jnp.tile(ok, (1, HPG))[None], bias(dist), NEG)

    i = jnp.arange(QB)[None, :]
    et = (jnp.arange(S)[:, None] // SEL_BLOCK == jnp.arange(n_s)[None, :]).astype(BF16)
    cmp_start = jnp.arange(n_cp) * CMP_STRIDE
    sel_start = jnp.arange(n_s) * SEL_BLOCK
    ovt = ((cmp_start[None, :] <= sel_start[:, None] + SEL_BLOCK - 1)
           & (cmp_start[None, :] + CMP_BLOCK - 1 >= sel_start[:, None])
           & (jnp.arange(n_cp)[None, :] < n_cp - 1)).astype(BF16)
    d_near = i + (NEAR_TILES - 1) * QB - jnp.arange(NEAR_TILES * QB)[:, None]
    bnear = masked(d_near, d_near >= 0)
    cfar = bias(jnp.full((1, QB), S + REL_MAX_DIST, jnp.int32))
    d_win = i + WINDOW - jnp.arange(WINDOW + QB)[:, None]
    bwin = masked(d_win, (d_win >= 0) & (d_win < WINDOW))
    d_cmp = i - CMP_STRIDE * (jnp.arange(2 * n_cp)[:, None] - n_cp) - (CMP_BLOCK - 1)
    bcmp = masked(d_cmp, d_cmp >= 0)
    return et, ovt, bnear, cfar, bwin, bcmp


def _retention_tables(S):
    H, DK, C = RET_HEADS, RET_DK, RET_CHUNK
    half = DK // 2
    pos = jnp.arange(S, dtype=F32)
    inv = ROPE_BASE ** (-jnp.arange(half, dtype=F32) / half)
    ang = pos[:, None] * inv[None, :]
    cos = jnp.tile(jnp.cos(ang), (1, H))
    sin = jnp.tile(jnp.sin(ang), (1, H))
    log_g = jnp.log1p(-jnp.exp2(-5.0 - jnp.arange(H, dtype=F32)))
    ar = jnp.arange(C)
    diff = ar[:, None] - ar[None, :]
    decay = jnp.where(diff >= 0, jnp.exp(log_g[:, None, None] * jnp.maximum(diff, 0).astype(F32)), 0.0)
    zeta = jnp.exp(log_g[:, None] * (C - 1 - ar).astype(F32))
    xi = jnp.exp(log_g[:, None] * (ar + 1).astype(F32))
    gch = jnp.exp(log_g * C)
    lane_head = (jnp.arange(H * DK) % (H * half)) // half
    hm = (lane_head[None, :] == jnp.arange(H)[:, None]).astype(F32)
    return (gch, cos, sin, decay, hm[:, None, :], xi[:, :, None] * hm[:, None, :],
            zeta[:, :, None] * hm[:, None, :])


def _pack_w_in(w):
    o = np.cumsum([0, 512, 128, 128, 128, 128, 128, 128, 24, 256, 256, 512, 512, 512, 512, 3072])
    seg = lambda i: w[:, int(o[i]):int(o[i + 1])]

    def halves_major(a):
        a4 = a.reshape(a.shape[0], RET_HEADS, 2, RET_DK // 2)
        return jnp.transpose(a4, (0, 2, 1, 3)).reshape(a.shape[0], RET_HEADS * RET_DK)

    parts = [seg(14), seg(0) * (NSA_DH ** -0.5), seg(10), seg(11), seg(12), seg(13),
             halves_major(seg(8)), halves_major(seg(9)) * (RET_DK ** -0.5),
             seg(1), seg(2), seg(3), seg(4), seg(5), seg(6),
             seg(7), jnp.zeros((w.shape[0], LANE - 3 * NSA_HEADS), w.dtype)]
    return jnp.concatenate(parts, axis=1).astype(BF16)


def _pack_compress(pe, w1, w2):
    G, DH, r = NSA_GROUPS, NSA_DH, CMP_BLOCK // CMP_STRIDE
    eye = jnp.eye(G, dtype=w1.dtype)
    pe2 = jnp.tile(pe.reshape(r, CMP_STRIDE, 1, DH), (1, 1, G, 1)).reshape(r, CMP_STRIDE * G * DH)
    w1r = w1.reshape(r, CMP_STRIDE, DH, CMP_HIDDEN)
    w1b = jnp.einsum('rldf,gk->rlgdkf', w1r, eye).reshape(r, CMP_STRIDE * G * DH, G * CMP_HIDDEN)
    w2b = jnp.einsum('fd,gk->gfkd', w2, eye).reshape(G * CMP_HIDDEN, G * DH)
    return pe2, w1b.astype(BF16), w2b.astype(BF16)


def kernel(x, rel_table, norm_mix, w_in, cmp_pe_k, cmp_w1_k, cmp_w2_k, cmp_pe_v, cmp_w1_v, cmp_w2_v, ret_gn, conv_w, conv_b, conv_ln_g, conv_ln_b, w_branch, w_out, norm_mlp, w_ff1, w_ff2, norm_final):
    B, S, D = x.shape
    depth = w_in.shape[0]
    T = B * S
    nsa_tabs = _nsa_tables(rel_table, S)
    ret_tabs = _retention_tables(S)
    x2 = x.reshape(T, D)
    for l in range(depth):
        z2 = _in_proj(x2, norm_mix[l][None, :], _pack_w_in(w_in[l]))
        z3 = z2.reshape(B, S, Z_COLS)
        n_ch = S // CMP_STRIDE
        kc2 = z3[:, :, COL_KC:COL_KC + NSA_KV].reshape(B, n_ch, CMP_STRIDE * NSA_KV)
        vc2 = z3[:, :, COL_KC + NSA_KV:COL_KC + 2 * NSA_KV].reshape(B, n_ch, CMP_STRIDE * NSA_KV)
        kcmp, vcmp = _compress(kc2, vc2, *_pack_compress(cmp_pe_k[l], cmp_w1_k[l], cmp_w2_k[l]),
                               *_pack_compress(cmp_pe_v[l], cmp_w1_v[l], cmp_w2_v[l]))
        kcmp = jnp.transpose(kcmp.reshape(B, n_ch, NSA_GROUPS, NSA_DH), (0, 2, 1, 3))
        vcmpt = jnp.transpose(vcmp.reshape(B, n_ch, NSA_GROUPS, NSA_DH), (0, 2, 3, 1))

        def cols_t(col, width, dtype):
            return jnp.swapaxes(z3[:, :, col:col + width], 1, 2).astype(dtype)

        o_nsa_t = _nsa_attention(z3, cols_t(COL_QN, BRANCH_W, BF16), cols_t(COL_GN, 32, F32),
                                 cols_t(COL_KC + 3 * NSA_KV, NSA_KV, BF16), cols_t(COL_KC + 5 * NSA_KV, NSA_KV, BF16),
                                 kcmp, vcmpt, nsa_tabs)
        o_nsa = jnp.swapaxes(o_nsa_t, 1, 2)
        o_ret = _retention(z3, ret_tabs, ret_gn[l][None, :])
        conv_w_pad = jnp.concatenate([conv_w[l], jnp.zeros((1, BRANCH_W), conv_w.dtype)], axis=0)
        o_conv = _conv_module(z3, conv_w_pad, conv_b[l][None, :], conv_ln_g[l][None, :], conv_ln_b[l][None, :])
        x2 = _merge(x2, z2, o_nsa.reshape(T, BRANCH_W), o_ret.reshape(T, BRANCH_W), o_conv.reshape(T, BRANCH_W),
                    w_branch[l].astype(BF16), w_out[l].astype(BF16))
        x2 = _ffn(x2, norm_mlp[l][None, :], w_ff1[l].astype(BF16), w_ff2[l].astype(BF16), norm_final[None, :],
                  final_norm=(l == depth - 1))
    return x2.reshape(B, S, D)
```

```python
import functools
import math

import jax
import jax.numpy as jnp
import numpy as np
from jax import lax
from jax.experimental import pallas as pl
from jax.experimental.pallas import tpu as pltpu

F32 = jnp.float32
BF16 = jnp.bfloat16

D_MODEL = 1024
BRANCH_W = 512
N_BRANCH = 3
NSA_HEADS = 8
NSA_GROUPS = 2
NSA_HPG = 4
NSA_DH = 64
NSA_KV = NSA_GROUPS * NSA_DH
CMP_BLOCK = 32
CMP_STRIDE = 16
CMP_HIDDEN = 128
SEL_BLOCK = 64
SEL_TOPK = 8
WINDOW = 512
Q_BLOCK = 128
RET_HEADS = 4
RET_DV = 128
RET_DK = 64
RET_CHUNK = 128
ROPE_BASE = 10000.0
CONV_WIDTH = 31
REL_BUCKETS = 32
REL_MAX_DIST = 128
D_FF = 4096
EPS = 1e-6
NEG = -1e30
LOG2E = 1.4426950408889634

LANE = 128
VMEM_LIMIT = 56 * 1024 * 1024

COL_QN = 0
COL_VR = 512
COL_GR = 1024
COL_CA = 1536
COL_CB = 2048
COL_QR = 2560
COL_KR = 2816
COL_KC = 3072
COL_GN = 3840
Z_COLS = 3968


def _cparams(sem):
    return pltpu.CompilerParams(dimension_semantics=sem, vmem_limit_bytes=VMEM_LIMIT)


def _rmsnorm_bf16(x, g):
    ms = jnp.mean(x * x, axis=-1, keepdims=True)
    return (x * lax.rsqrt(ms + EPS) * g).astype(BF16)


def _in_proj_body(x_ref, g_ref, w_ref, z_ref):
    z_ref[...] = jnp.dot(_rmsnorm_bf16(x_ref[...], g_ref[...]), w_ref[...], preferred_element_type=F32)


def _in_proj(x2, g, w):
    T = x2.shape[0]
    tm = 512
    return pl.pallas_call(
        _in_proj_body,
        grid=(T // tm,),
        in_specs=[pl.BlockSpec((tm, D_MODEL), lambda i: (i, 0)),
                  pl.BlockSpec((1, D_MODEL), lambda i: (0, 0)),
                  pl.BlockSpec((D_MODEL, Z_COLS), lambda i: (0, 0))],
        out_specs=pl.BlockSpec((tm, Z_COLS), lambda i: (i, 0)),
        out_shape=jax.ShapeDtypeStruct((T, Z_COLS), F32),
        compiler_params=_cparams(("parallel",)),
        name="in_proj",
    )(x2, g, w)


def _compress_body(kc_ref, vc_ref, pek_ref, w1k_ref, w2k_ref, pev_ref, w1v_ref, w2v_ref, ko_ref, vo_ref):
    def one(c_ref, pe_ref, w1_ref, w2_ref, o_ref):
        ch = c_ref[...]
        a = jnp.dot((ch + pe_ref[0:1, :]).astype(BF16), w1_ref[0], preferred_element_type=F32)
        b = jnp.dot((ch + pe_ref[1:2, :]).astype(BF16), w1_ref[1], preferred_element_type=F32)
        n = b.shape[0]
        hid = jax.nn.gelu(a + pltpu.roll(b, n - 1, 0))
        o_ref[...] = jnp.dot(hid.astype(BF16), w2_ref[...], preferred_element_type=F32).astype(o_ref.dtype)

    one(kc_ref, pek_ref, w1k_ref, w2k_ref, ko_ref)
    one(vc_ref, pev_ref, w1v_ref, w2v_ref, vo_ref)


def _compress(kc2, vc2, pek, w1k, w2k, pev, w1v, w2v):
    B, n_ch, wide = kc2.shape
    hid2 = NSA_GROUPS * CMP_HIDDEN
    act = pl.BlockSpec((None, n_ch, wide), lambda b: (b, 0, 0))
    pe = pl.BlockSpec((2, wide), lambda b: (0, 0))
    w1 = pl.BlockSpec((2, wide, hid2), lambda b: (0, 0, 0))
    w2 = pl.BlockSpec((hid2, NSA_KV), lambda b: (0, 0))
    out = pl.BlockSpec((None, n_ch, NSA_KV), lambda b: (b, 0, 0))
    return pl.pallas_call(
        _compress_body,
        grid=(B,),
        in_specs=[act, act, pe, w1, w2, pe, w1, w2],
        out_specs=[out, out],
        out_shape=[jax.ShapeDtypeStruct((B, n_ch, NSA_KV), BF16)] * 2,
        compiler_params=_cparams(("parallel",)),
        name="nsa_compress",
    )(kc2, vc2, pek, w1k, w2k, pev, w1v, w2v)


SEL_TILE = 2 * Q_BLOCK
V_ROWS = NSA_DH + 16
NEAR_TILES = 3


def _nsa_body(q_ref, gn_ref, ks_ref, vs_ref, kw_ref, vw_ref, kcmp_ref, vcmpt_ref,
              et_ref, ovt_ref, bnear_ref, cfar_ref, bwin_ref, bcmp_ref, o_ref,
              kaug_scr, kw_scr, vst_ref, vwt_ref, part_scr, sa_scr, sb_scr, pa_scr, pb_scr):
    c = pl.program_id(1)
    n_s = et_ref.shape[1]
    n_cp = kcmp_ref.shape[1]
    QB, DH, HPG = Q_BLOCK, NSA_DH, NSA_HPG
    s_len = ks_ref.shape[0]

    @pl.when(c == 0)
    def _():
        for g in range(NSA_GROUPS):
            kaug_scr[g, 0:s_len, 0:DH] = ks_ref[:, g * DH:(g + 1) * DH].astype(BF16)
            kaug_scr[g, 0:s_len, DH:DH + n_s] = et_ref[...]
            kaug_scr[g, s_len:s_len + SEL_TILE, 0:DH] = jnp.zeros((SEL_TILE, DH), BF16)
            kaug_scr[g, s_len:s_len + SEL_TILE, DH:DH + n_s] = jnp.ones((SEL_TILE, n_s), BF16)
            kw_scr[g] = kw_ref[:, g * DH:(g + 1) * DH].astype(BF16)

        def transpose_values(t, _):
            r0 = pl.multiple_of(t * QB, QB)
            for src, dst in ((vs_ref, vst_ref), (vw_ref, vwt_ref)):
                v_t = src[pl.ds(r0, QB), :].T.astype(BF16)
                for g in range(NSA_GROUPS):
                    dst[g, 0:DH, pl.ds(r0, QB)] = v_t[g * DH:(g + 1) * DH]
            return 0

        lax.fori_loop(0, s_len // QB, transpose_values, 0)
        ones_row = (lax.broadcasted_iota(jnp.int32, (V_ROWS - DH, s_len), 0) == 0).astype(BF16)
        for g in range(NSA_GROUPS):
            vst_ref[g, DH:V_ROWS, :] = ones_row
            vwt_ref[g, DH:V_ROWS, :] = ones_row

    gates = jax.nn.sigmoid(gn_ref[...].T)
    q_t = [q_ref[:, i * LANE:(i + 1) * LANE].T for i in range(BRANCH_W // LANE)]

    blk = lax.broadcasted_iota(jnp.int32, (n_s, QB), 0)
    cur = (c * QB + lax.broadcasted_iota(jnp.int32, (n_s, QB), 1)) // SEL_BLOCK
    valid = blk <= cur
    forced = (blk == 0) | (blk == cur) | (blk == cur - 1)
    blk_f = blk.astype(F32)

    cmp_off = pl.multiple_of(n_cp - c * (QB // CMP_STRIDE), QB // CMP_STRIDE)
    n_far = jnp.maximum(c - 1, 0)
    n_w = WINDOW // QB + 1

    def normalised(acc):
        return acc[0:DH] * (1.0 / jnp.maximum(acc[DH:DH + 1], 1e-30))

    def gate_row(g, branch):
        return jnp.concatenate([gates[3 * (g * HPG + h) + branch:3 * (g * HPG + h) + branch + 1, :]
                                for h in range(HPG)], axis=1)

    qas = []
    for g in range(NSA_GROUPS):
        qt = jnp.concatenate([q_t[(g * HPG + h) * DH // LANE][((g * HPG + h) * DH) % LANE:((g * HPG + h) * DH) % LANE + DH]
                              for h in range(HPG)], axis=1).astype(BF16)

        s_c = (jnp.dot(kcmp_ref[g], qt, preferred_element_type=F32)
               + bcmp_ref[g, pl.ds(cmp_off, n_cp), :])
        e_c = jnp.where(s_c > 0.5 * NEG, jnp.exp2(s_c - jnp.max(s_c, axis=0, keepdims=True)), 0.0)
        p_c = e_c * (1.0 / jnp.maximum(jnp.sum(e_c, axis=0, keepdims=True), 1e-30))
        o_c = jnp.dot(vcmpt_ref[g], p_c.astype(BF16), preferred_element_type=F32)

        p_sum = p_c[:, 0:QB] + p_c[:, QB:2 * QB] + p_c[:, 2 * QB:3 * QB] + p_c[:, 3 * QB:4 * QB]
        p_hi = p_sum.astype(BF16)
        p_lo = (p_sum - p_hi.astype(F32)).astype(BF16)
        imp = (jnp.dot(ovt_ref[...], p_hi, preferred_element_type=F32)
               + jnp.dot(ovt_ref[...], p_lo, preferred_element_type=F32))
        score = jnp.where(forced, imp + 1e4, jnp.where(valid, imp, -1e4))
        selneg = jnp.full((n_s, QB), NEG, F32)
        for _ in range(min(SEL_TOPK, n_s)):
            m = jnp.max(score, axis=0, keepdims=True)
            first = jnp.min(jnp.where(score == m, blk_f, float(n_s)), axis=0, keepdims=True)
            pick = blk_f == first
            selneg = jnp.where(pick, 0.0, selneg)
            score = jnp.where(pick, -jnp.inf, score)
        selneg = selneg.astype(BF16)

        qas.append(jnp.concatenate([qt, jnp.concatenate([selneg] * HPG, axis=1)], axis=0))

        w_tiles = [c - (n_w - 1) + w for w in range(n_w)]
        w_starts = [pl.multiple_of(jnp.maximum(t, 0) * QB, QB) for t in w_tiles]
        k_w = jnp.concatenate([kw_scr[g, pl.ds(r0, QB), :] for r0 in w_starts], axis=0)
        v_w = jnp.concatenate([vwt_ref[g, :, pl.ds(r0, QB)] for r0 in w_starts], axis=1)
        s_w = jnp.dot(k_w, qt, preferred_element_type=F32) + bwin_ref[g]
        s_w = jnp.concatenate([s_w[w * QB:(w + 1) * QB] + jnp.where(w_tiles[w] >= 0, 0.0, NEG)
                               for w in range(n_w - 1)] + [s_w[(n_w - 1) * QB:]], axis=0)
        e_w = jnp.exp2(s_w - jnp.max(s_w, axis=0, keepdims=True))
        o_w = normalised(jnp.dot(v_w, e_w.astype(BF16), preferred_element_type=F32))

        part_scr[g] = gate_row(g, 0) * o_c + gate_row(g, 2) * o_w

    n_it = n_far // 2
    groups = range(NSA_GROUPS)

    def scores(kt, g, dst):
        r0 = pl.multiple_of(jnp.where(kt < n_it, kt * SEL_TILE, s_len), SEL_TILE)
        dst[g] = jnp.dot(kaug_scr[g, pl.ds(r0, SEL_TILE), :], qas[g], preferred_element_type=F32)

    def values(kt, g, p_src):
        r0 = pl.multiple_of(jnp.clip(kt, 0, jnp.maximum(n_it - 1, 0)) * SEL_TILE, SEL_TILE)
        return jnp.dot(vst_ref[g, :, pl.ds(r0, SEL_TILE)], p_src[g], preferred_element_type=F32)

    def half(kt, carries, src, dst, p_src, p_dst):
        for g in groups:
            scores(kt + 1, g, dst)
        pv = [values(kt - 1, g, p_src) for g in groups]
        out = []
        for g in groups:
            m_i, acc, alpha_prev = carries[g]
            s = src[g]
            m_n = jnp.maximum(m_i, jnp.max(s, axis=0, keepdims=True))
            p_dst[g] = jnp.exp2(s - m_n).astype(BF16)
            out.append((m_n, alpha_prev * acc + pv[g], jnp.exp2(m_i - m_n)))
        return tuple(out)

    def far(j, carries):
        first = half(2 * j, carries, sa_scr, sb_scr, pb_scr, pa_scr)
        return half(2 * j + 1, first, sb_scr, sa_scr, pa_scr, pb_scr)

    for g in groups:
        pb_scr[g] = jnp.zeros((SEL_TILE, HPG * QB), BF16)
        scores(0, g, sa_scr)
    init = (jnp.full((1, HPG * QB), NEG, F32), jnp.zeros((V_ROWS, HPG * QB), F32), jnp.ones((1, HPG * QB), F32))
    n_trips = (n_it + 1) // 2
    piped = lax.fori_loop(0, n_trips, far, (init,) * NSA_GROUPS)
    carries = [(m_i, alpha_prev * acc + values(2 * n_trips - 1, g, pb_scr))
               for g, (m_i, acc, alpha_prev) in enumerate(piped)]

    tiles = (jnp.maximum(n_far - 1, 0), jnp.maximum(c - 1, 0), c)
    pens = (jnp.where(n_far % 2 == 1, 0.0, NEG), jnp.where(c >= 1, 0.0, NEG), None)
    starts = [pl.multiple_of(t * QB, QB) for t in tiles]
    for g in range(NSA_GROUPS):
        m_f, acc_f = carries[g]
        m_f = m_f + cfar_ref[g]
        k_n = jnp.concatenate([kaug_scr[g, pl.ds(r0, QB), :] for r0 in starts], axis=0)
        v_n = jnp.concatenate([vst_ref[g, :, pl.ds(r0, QB)] for r0 in starts], axis=1)
        s_n = jnp.dot(k_n, qas[g], preferred_element_type=F32) + bnear_ref[g]
        s_n = jnp.concatenate([s_n[i * QB:(i + 1) * QB] if pen is None else s_n[i * QB:(i + 1) * QB] + pen
                               for i, pen in enumerate(pens)], axis=0)
        m_s = jnp.maximum(m_f, jnp.max(s_n, axis=0, keepdims=True))
        acc_s = jnp.exp2(m_f - m_s) * acc_f + jnp.dot(v_n, jnp.exp2(s_n - m_s).astype(BF16),
                                                      preferred_element_type=F32)
        o_g = part_scr[g] + gate_row(g, 1) * normalised(acc_s)
        for i in range(HPG * DH // LANE):
            pair = jnp.concatenate([o_g[:, (2 * i) * QB:(2 * i + 1) * QB], o_g[:, (2 * i + 1) * QB:(2 * i + 2) * QB]], axis=0)
            col = g * HPG * DH + i * LANE
            o_ref[:, col:col + LANE] = pair.T.astype(o_ref.dtype)


def _nsa_attention(z3, kcmp, vcmpt, tabs):
    B, S, _ = z3.shape
    et, ovt, bnear, cfar, bwin, bcmp = tabs

    def zcol(width, col):
        return pl.BlockSpec((None, Q_BLOCK, width), lambda b, c: (b, c, col // width))

    def zfull(col):
        return pl.BlockSpec((None, S, NSA_KV), lambda b, c: (b, 0, col // NSA_KV))

    def per_batch(a):
        nd = a.ndim - 1
        return pl.BlockSpec((None,) + a.shape[1:], lambda b, c: (b,) + (0,) * nd)

    def const(a):
        nd = a.ndim
        return pl.BlockSpec(a.shape, lambda b, c: (0,) * nd)

    tile = (NSA_GROUPS, SEL_TILE, NSA_HPG * Q_BLOCK)
    return pl.pallas_call(
        _nsa_body,
        grid=(B, S // Q_BLOCK),
        in_specs=[zcol(BRANCH_W, COL_QN), zcol(LANE, COL_GN),
                  zfull(COL_KC + 2 * NSA_KV), zfull(COL_KC + 3 * NSA_KV),
                  zfull(COL_KC + 4 * NSA_KV), zfull(COL_KC + 5 * NSA_KV),
                  per_batch(kcmp), per_batch(vcmpt),
                  const(et), const(ovt), const(bnear), const(cfar), const(bwin), const(bcmp)],
        out_specs=pl.BlockSpec((None, Q_BLOCK, BRANCH_W), lambda b, c: (b, c, 0)),
        out_shape=jax.ShapeDtypeStruct((B, S, BRANCH_W), BF16),
        scratch_shapes=[pltpu.VMEM((NSA_GROUPS, S + SEL_TILE, NSA_DH + et.shape[1]), BF16),
                        pltpu.VMEM((NSA_GROUPS, S, NSA_DH), BF16),
                        pltpu.VMEM((NSA_GROUPS, V_ROWS, S), BF16), pltpu.VMEM((NSA_GROUPS, V_ROWS, S), BF16),
                        pltpu.VMEM((NSA_GROUPS, NSA_DH, NSA_HPG * Q_BLOCK), F32),
                        pltpu.VMEM(tile, F32), pltpu.VMEM(tile, F32), pltpu.VMEM(tile, BF16), pltpu.VMEM(tile, BF16)],
        compiler_params=_cparams(("parallel", "arbitrary")),
        name="nsa_attention",
    )(z3, z3, z3, z3, z3, z3, kcmp, vcmpt, et, ovt, bnear, cfar, bwin, bcmp)


def _retention_body(gch_ref, q_ref, k_ref, v_ref, g_ref, cos_ref, sin_ref, decay_ref, hm_ref, xi_ref,
                    zeta_ref, gn_ref, o_ref, state_scr):
    @pl.when(pl.program_id(1) == 0)
    def _():
        state_scr[...] = jnp.zeros_like(state_scr)

    half = RET_HEADS * RET_DK // 2
    cos, sin = cos_ref[...], sin_ref[...]

    def rot(x):
        x1, x2 = x[:, :half], x[:, half:]
        return jnp.concatenate([x1 * cos - x2 * sin, x1 * sin + x2 * cos], axis=1)

    qr = rot(q_ref[...])
    kr = rot(k_ref[...])
    kb = kr.astype(BF16)
    nt = (((1,), (1,)), ((), ()))
    tn = (((0,), (0,)), ((), ()))
    for h in range(RET_HEADS):
        cols = slice(h * RET_DV, (h + 1) * RET_DV)
        vh = v_ref[:, cols].astype(BF16)
        inner = lax.dot_general((qr * hm_ref[h]).astype(BF16), kb, nt, preferred_element_type=F32) * decay_ref[h]
        o = jnp.dot(inner.astype(BF16), vh, preferred_element_type=F32)
        state = state_scr[h]
        o = o + jnp.dot((qr * xi_ref[h]).astype(BF16), state.astype(BF16), preferred_element_type=F32)
        kv = lax.dot_general((kr * zeta_ref[h]).astype(BF16), vh, tn, preferred_element_type=F32)
        state_scr[h] = gch_ref[h] * state + kv
        mu = jnp.mean(o, axis=-1, keepdims=True)
        d = o - mu
        var = jnp.mean(d * d, axis=-1, keepdims=True)
        y = d * lax.rsqrt(var + EPS) * gn_ref[:, cols]
        gate = g_ref[:, cols]
        o_ref[:, cols] = (gate * jax.nn.sigmoid(gate) * y).astype(o_ref.dtype)


def _retention(z3, tabs, gn):
    B, S, _ = z3.shape
    C = RET_CHUNK
    gch, cos, sin, decay, hm, xi, zeta = tabs
    qk_w = RET_HEADS * RET_DK

    def zcol(width, col):
        return pl.BlockSpec((None, C, width), lambda b, n: (b, n, col // width))

    def const(a):
        nd = a.ndim
        return pl.BlockSpec(a.shape, lambda b, n: (0,) * nd)

    pos = pl.BlockSpec((C, qk_w // 2), lambda b, n: (n, 0))
    return pl.pallas_call(
        _retention_body,
        grid=(B, S // C),
        in_specs=[pl.BlockSpec(memory_space=pltpu.SMEM),
                  zcol(qk_w, COL_QR), zcol(qk_w, COL_KR), zcol(BRANCH_W, COL_VR), zcol(BRANCH_W, COL_GR),
                  pos, pos, const(decay), const(hm), const(xi), const(zeta), const(gn)],
        out_specs=pl.BlockSpec((None, C, BRANCH_W), lambda b, n: (b, n, 0)),
        out_shape=jax.ShapeDtypeStruct((B, S, BRANCH_W), BF16),
        scratch_shapes=[pltpu.VMEM((RET_HEADS, qk_w, RET_DV), F32)],
        compiler_params=_cparams(("parallel", "arbitrary")),
        name="retention",
    )(gch, z3, z3, z3, z3, cos, sin, decay, hm, xi, zeta, gn)


CONV_HALO = 32
SUBLANES = 8
CONV_ROWS = 64


def _conv_body(a_ref, b_ref, w_ref, cb_ref, lg_ref, lb_ref, o_ref, u_scr):
    tile = a_ref.shape[0]
    rows = CONV_HALO + tile

    @pl.when(pl.program_id(1) == 0)
    def _():
        u_scr[0, 0:CONV_HALO, :] = jnp.zeros((CONV_HALO, u_scr.shape[2]), F32)

    @pl.when(pl.program_id(1) > 0)
    def _():
        u_scr[0, 0:CONV_HALO, :] = u_scr[0, tile:rows, :]

    u_scr[0, CONV_HALO:rows, :] = a_ref[...] * jax.nn.sigmoid(b_ref[...])
    for r in range(1, SUBLANES):
        u_scr[r, 0:rows - SUBLANES, :] = u_scr[0, r:rows - SUBLANES + r, :]
    lead = CONV_HALO - (CONV_WIDTH - 1)

    for r0 in range(0, tile, CONV_ROWS):
        acc = jnp.zeros((CONV_ROWS, u_scr.shape[2]), F32) + cb_ref[...]
        for k in range(CONV_WIDTH):
            start = r0 + (lead + k) // SUBLANES * SUBLANES
            acc = acc + w_ref[k:k + 1, :] * u_scr[(lead + k) % SUBLANES, start:start + CONV_ROWS, :]
        mu = jnp.mean(acc, axis=-1, keepdims=True)
        d = acc - mu
        var = jnp.mean(d * d, axis=-1, keepdims=True)
        y = d * lax.rsqrt(var + EPS) * lg_ref[...] + lb_ref[...]
        o_ref[r0:r0 + CONV_ROWS, :] = (y * jax.nn.sigmoid(y)).astype(o_ref.dtype)


def _conv_module(z3, w, cb, lg, lb):
    B, S, _ = z3.shape
    tile = min(256, S)

    def zcol(col):
        return pl.BlockSpec((None, tile, BRANCH_W), lambda b, i: (b, i, col // BRANCH_W))

    def const(a):
        return pl.BlockSpec(a.shape, lambda b, i: (0, 0))

    return pl.pallas_call(
        _conv_body,
        grid=(B, S // tile),
        in_specs=[zcol(COL_CA), zcol(COL_CB), const(w), const(cb), const(lg), const(lb)],
        out_specs=pl.BlockSpec((None, tile, BRANCH_W), lambda b, i: (b, i, 0)),
        out_shape=jax.ShapeDtypeStruct((B, S, BRANCH_W), BF16),
        scratch_shapes=[pltpu.VMEM((SUBLANES, CONV_HALO + tile, BRANCH_W), F32)],
        compiler_params=_cparams(("parallel", "arbitrary")),
        name="conv_module",
    )(z3, z3, w, cb, lg, lb)


def _merge_body(x_ref, g_ref, wg_ref, a_ref, r_ref, c_ref, wb_ref, wo_ref, o_ref):
    x = x_ref[...]
    h = _rmsnorm_bf16(x, g_ref[...])
    merged = None
    for i, br in enumerate((a_ref, r_ref, c_ref)):
        gate = jnp.dot(h, wg_ref[:, i * D_MODEL:(i + 1) * D_MODEL], preferred_element_type=F32)
        proj = jnp.dot(br[...], wb_ref[i], preferred_element_type=F32)
        term = jax.nn.sigmoid(gate) * proj
        merged = term if merged is None else merged + term
    o_ref[...] = x + jnp.dot(merged.astype(BF16), wo_ref[...], preferred_element_type=F32)


def _merge(x2, g, wg, o_nsa, o_ret, o_conv, wb, wo):
    T = x2.shape[0]
    tm = 512
    br = pl.BlockSpec((tm, BRANCH_W), lambda i: (i, 0))
    return pl.pallas_call(
        _merge_body,
        grid=(T // tm,),
        in_specs=[pl.BlockSpec((tm, D_MODEL), lambda i: (i, 0)),
                  pl.BlockSpec((1, D_MODEL), lambda i: (0, 0)),
                  pl.BlockSpec(wg.shape, lambda i: (0, 0)),
                  br, br, br,
                  pl.BlockSpec(wb.shape, lambda i: (0, 0, 0)),
                  pl.BlockSpec(wo.shape, lambda i: (0, 0))],
        out_specs=pl.BlockSpec((tm, D_MODEL), lambda i: (i, 0)),
        out_shape=jax.ShapeDtypeStruct((T, D_MODEL), F32),
        compiler_params=_cparams(("parallel",)),
        name="merge_out_proj",
    )(x2, g, wg, o_nsa, o_ret, o_conv, wb, wo)


FF_CHUNK = 1024


def _ffn_body(x_ref, g_ref, w1_ref, w2_ref, gf_ref, o_ref, *, final_norm):
    x = x_ref[...]
    ms = jnp.mean(x * x, axis=-1, keepdims=True)
    h = (x * lax.rsqrt(ms + EPS) * g_ref[...]).astype(BF16)
    acc = x
    for j in range(D_FF // FF_CHUNK):
        cols = slice(j * FF_CHUNK, (j + 1) * FF_CHUNK)
        a = jnp.maximum(jnp.dot(h, w1_ref[:, cols], preferred_element_type=F32), 0.0)
        acc = acc + jnp.dot((a * a).astype(BF16), w2_ref[cols, :], preferred_element_type=F32)
    if final_norm:
        ms = jnp.mean(acc * acc, axis=-1, keepdims=True)
        acc = acc * lax.rsqrt(ms + EPS) * gf_ref[...]
    o_ref[...] = acc


def _ffn(x2, g, w1, w2, gf, final_norm):
    T = x2.shape[0]
    tm = 512
    vec = pl.BlockSpec((1, D_MODEL), lambda i: (0, 0))
    return pl.pallas_call(
        functools.partial(_ffn_body, final_norm=final_norm),
        grid=(T // tm,),
        in_specs=[pl.BlockSpec((tm, D_MODEL), lambda i: (i, 0)), vec,
                  pl.BlockSpec(w1.shape, lambda i: (0, 0)),
                  pl.BlockSpec(w2.shape, lambda i: (0, 0)), vec],
        out_specs=pl.BlockSpec((tm, D_MODEL), lambda i: (i, 0)),
        out_shape=jax.ShapeDtypeStruct((T, D_MODEL), F32),
        compiler_params=_cparams(("parallel",)),
        name="ffn",
    )(x2, g, w1, w2, gf)


def _rel_bucket(dist):
    n = jnp.maximum(dist, 0)
    max_exact = REL_BUCKETS // 2
    nf = jnp.maximum(n, 1).astype(F32)
    large = max_exact + (jnp.log(nf / max_exact) / math.log(REL_MAX_DIST / max_exact)
                         * (REL_BUCKETS - max_exact)).astype(jnp.int32)
    large = jnp.minimum(large, REL_BUCKETS - 1)
    return jnp.where(n < max_exact, n, large)


def _nsa_tables(rel_table, S):
    n_s = S // SEL_BLOCK
    n_cp = S // CMP_STRIDE
    QB, G, HPG = Q_BLOCK, NSA_GROUPS, NSA_HPG
    tab = rel_table.T.astype(F32)

    def bias_of(dist):
        onehot = _rel_bucket(dist.reshape(-1))[None, :, None] == jnp.arange(REL_BUCKETS)
        return jnp.sum(jnp.where(onehot, tab[:, None, :], 0.0), axis=-1).reshape((NSA_HEADS,) + dist.shape)

    def skew(v, rows, cols):
        n = v.shape[-1]
        flat = jnp.tile(v, (1,) * (v.ndim - 1) + (rows,))[..., :rows * (n - 1)]
        return flat.reshape(v.shape[:-1] + (rows, n - 1))[..., :cols]

    def signed(n, cols):
        k = jnp.arange(n)
        return jnp.where(k < cols, k, k - n)

    def stack(full):
        J = full.shape[1]
        return jnp.transpose(full.reshape(G, HPG, J, QB), (0, 2, 1, 3)).reshape(G, J, HPG * QB)

    def band(rows, off, ok):
        d = signed(rows + QB, QB) + off
        return stack(skew(jnp.where(ok(d)[None, :], bias_of(d) * LOG2E, NEG), rows, QB))

    et = (jnp.arange(S)[:, None] // SEL_BLOCK == jnp.arange(n_s)[None, :]).astype(BF16)
    cmp_start = jnp.arange(n_cp) * CMP_STRIDE
    sel_start = jnp.arange(n_s) * SEL_BLOCK
    ovt = ((cmp_start[None, :] <= sel_start[:, None] + SEL_BLOCK - 1)
           & (cmp_start[None, :] + CMP_BLOCK - 1 >= sel_start[:, None])
           & (jnp.arange(n_cp)[None, :] < n_cp - 1)).astype(BF16)
    bnear = band(NEAR_TILES * QB, (NEAR_TILES - 1) * QB, lambda d: d >= 0)
    bwin = band(WINDOW + QB, WINDOW, lambda d: (d >= 0) & (d < WINDOW))
    far = bias_of(jnp.full((1,), S + REL_MAX_DIST, jnp.int32))
    cfar = jnp.broadcast_to(far.reshape(G, 1, HPG, 1) * LOG2E, (G, 1, HPG, QB)).reshape(G, 1, HPG * QB)
    per = QB // CMP_STRIDE
    rows = 2 * n_cp
    d_cmp = (CMP_STRIDE * (signed(rows + per, per)[None, :] + n_cp) + jnp.arange(CMP_STRIDE)[:, None]
             - (CMP_BLOCK - 1))
    sk = skew(jnp.where((d_cmp >= 0)[None], bias_of(d_cmp) * LOG2E, NEG), rows, per)
    bcmp = stack(jnp.transpose(sk, (0, 2, 3, 1)).reshape(NSA_HEADS, rows, QB))
    return et, ovt, bnear, cfar, bwin, bcmp


def _retention_tables(S):
    H, DK, C = RET_HEADS, RET_DK, RET_CHUNK
    half = DK // 2
    pos = jnp.arange(S, dtype=F32)
    inv = ROPE_BASE ** (-jnp.arange(half, dtype=F32) / half)
    ang = pos[:, None] * inv[None, :]
    cos = jnp.tile(jnp.cos(ang), (1, H))
    sin = jnp.tile(jnp.sin(ang), (1, H))
    log_g = jnp.log1p(-jnp.exp2(-5.0 - jnp.arange(H, dtype=F32)))
    ar = jnp.arange(C)
    diff = ar[:, None] - ar[None, :]
    decay = jnp.where(diff >= 0, jnp.exp(log_g[:, None, None] * jnp.maximum(diff, 0).astype(F32)), 0.0)
    zeta = jnp.exp(log_g[:, None] * (C - 1 - ar).astype(F32))
    xi = jnp.exp(log_g[:, None] * (ar + 1).astype(F32))
    gch = jnp.exp(log_g * C)
    lane_head = (jnp.arange(H * DK) % (H * half)) // half
    hm = (lane_head[None, :] == jnp.arange(H)[:, None]).astype(F32)
    return (gch, cos, sin, decay, hm[:, None, :], xi[:, :, None] * hm[:, None, :],
            zeta[:, :, None] * hm[:, None, :])


def _pack_w_in(w):
    o = np.cumsum([0, 512, 128, 128, 128, 128, 128, 128, 24, 256, 256, 512, 512, 512, 512, 3072])
    seg = lambda i: w[:, int(o[i]):int(o[i + 1])]

    def halves_major(a):
        a4 = a.reshape(a.shape[0], RET_HEADS, 2, RET_DK // 2)
        return jnp.transpose(a4, (0, 2, 1, 3)).reshape(a.shape[0], RET_HEADS * RET_DK)

    parts = [seg(0) * (NSA_DH ** -0.5 * LOG2E), seg(10), seg(11), seg(12), seg(13),
             halves_major(seg(8)), halves_major(seg(9)) * (RET_DK ** -0.5),
             seg(1), seg(2), seg(3), seg(4), seg(5), seg(6),
             seg(7), jnp.zeros((w.shape[0], LANE - 3 * NSA_HEADS), w.dtype)]
    return jnp.concatenate(parts, axis=1).astype(BF16), seg(14).astype(BF16)


def _pack_compress(pe, w1, w2):
    G, DH, r = NSA_GROUPS, NSA_DH, CMP_BLOCK // CMP_STRIDE
    eye = jnp.eye(G, dtype=w1.dtype)
    pe2 = jnp.tile(pe.reshape(r, CMP_STRIDE, 1, DH), (1, 1, G, 1)).reshape(r, CMP_STRIDE * G * DH)
    w1r = w1.reshape(r, CMP_STRIDE, DH, CMP_HIDDEN)
    w1b = jnp.einsum('rldf,gk->rlgdkf', w1r, eye).reshape(r, CMP_STRIDE * G * DH, G * CMP_HIDDEN)
    w2b = jnp.einsum('fd,gk->gfkd', w2, eye).reshape(G * CMP_HIDDEN, G * DH)
    return pe2, w1b.astype(BF16), w2b.astype(BF16)


def kernel(x, rel_table, norm_mix, w_in, cmp_pe_k, cmp_w1_k, cmp_w2_k, cmp_pe_v, cmp_w1_v, cmp_w2_v, ret_gn, conv_w, conv_b, conv_ln_g, conv_ln_b, w_branch, w_out, norm_mlp, w_ff1, w_ff2, norm_final):
    B, S, D = x.shape
    depth = w_in.shape[0]
    T = B * S
    nsa_tabs = _nsa_tables(rel_table, S)
    ret_tabs = _retention_tables(S)
    x2 = x.reshape(T, D)
    for l in range(depth):
        w_main, w_gate = _pack_w_in(w_in[l])
        z2 = _in_proj(x2, norm_mix[l][None, :], w_main)
        z3 = z2.reshape(B, S, Z_COLS)
        n_ch = S // CMP_STRIDE
        kc2 = z3[:, :, COL_KC:COL_KC + NSA_KV].reshape(B, n_ch, CMP_STRIDE * NSA_KV)
        vc2 = z3[:, :, COL_KC + NSA_KV:COL_KC + 2 * NSA_KV].reshape(B, n_ch, CMP_STRIDE * NSA_KV)
        kcmp, vcmp = _compress(kc2, vc2, *_pack_compress(cmp_pe_k[l], cmp_w1_k[l], cmp_w2_k[l]),
                               *_pack_compress(cmp_pe_v[l], cmp_w1_v[l], cmp_w2_v[l]))
        kcmp = jnp.transpose(kcmp.reshape(B, n_ch, NSA_GROUPS, NSA_DH), (0, 2, 1, 3))
        vcmpt = jnp.transpose(vcmp.reshape(B, n_ch, NSA_GROUPS, NSA_DH), (0, 2, 3, 1))

        o_nsa = _nsa_attention(z3, kcmp, vcmpt, nsa_tabs)
        o_ret = _retention(z3, ret_tabs, ret_gn[l][None, :])
        conv_w_pad = jnp.concatenate([conv_w[l], jnp.zeros((1, BRANCH_W), conv_w.dtype)], axis=0)
        o_conv = _conv_module(z3, conv_w_pad, conv_b[l][None, :], conv_ln_g[l][None, :], conv_ln_b[l][None, :])
        x2 = _merge(x2, norm_mix[l][None, :], w_gate, o_nsa.reshape(T, BRANCH_W), o_ret.reshape(T, BRANCH_W), o_conv.reshape(T, BRANCH_W),
                    w_branch[l].astype(BF16), w_out[l].astype(BF16))
        x2 = _ffn(x2, norm_mlp[l][None, :], w_ff1[l].astype(BF16), w_ff2[l].astype(BF16), norm_final[None, :],
                  final_norm=(l == depth - 1))
    return x2.reshape(B, S, D)
```

```python
import functools
import math

import jax
import jax.numpy as jnp
import numpy as np
from jax import lax
from jax.experimental import pallas as pl
from jax.experimental.pallas import tpu as pltpu

F32 = jnp.float32
BF16 = jnp.bfloat16

D_MODEL = 1024
BRANCH_W = 512
N_BRANCH = 3
NSA_HEADS = 8
NSA_GROUPS = 2
NSA_HPG = 4
NSA_DH = 64
NSA_KV = NSA_GROUPS * NSA_DH
CMP_BLOCK = 32
CMP_STRIDE = 16
CMP_HIDDEN = 128
SEL_BLOCK = 64
SEL_TOPK = 8
WINDOW = 512
Q_BLOCK = 128
RET_HEADS = 4
RET_DV = 128
RET_DK = 64
RET_CHUNK = 128
ROPE_BASE = 10000.0
CONV_WIDTH = 31
REL_BUCKETS = 32
REL_MAX_DIST = 128
D_FF = 4096
EPS = 1e-6
NEG = -1e30
LOG2E = 1.4426950408889634

LANE = 128
VMEM_LIMIT = 56 * 1024 * 1024

COL_QN = 0
COL_VR = 512
COL_GR = 1024
COL_CA = 1536
COL_CB = 2048
COL_QR = 2560
COL_KR = 2816
COL_KC = 3072
COL_GN = 3840
Z_COLS = 3968


def _cparams(sem):
    return pltpu.CompilerParams(dimension_semantics=sem, vmem_limit_bytes=VMEM_LIMIT)


def _rmsnorm_bf16(x, g):
    ms = jnp.mean(x * x, axis=-1, keepdims=True)
    return (x * lax.rsqrt(ms + EPS) * g).astype(BF16)


def _in_proj_body(x_ref, g_ref, w_ref, z_ref):
    z_ref[...] = jnp.dot(_rmsnorm_bf16(x_ref[...], g_ref[...]), w_ref[...], preferred_element_type=F32)


def _in_proj(x2, g, w):
    T = x2.shape[0]
    tm = 512
    return pl.pallas_call(
        _in_proj_body,
        grid=(T // tm,),
        in_specs=[pl.BlockSpec((tm, D_MODEL), lambda i: (i, 0)),
                  pl.BlockSpec((1, D_MODEL), lambda i: (0, 0)),
                  pl.BlockSpec((D_MODEL, Z_COLS), lambda i: (0, 0))],
        out_specs=pl.BlockSpec((tm, Z_COLS), lambda i: (i, 0)),
        out_shape=jax.ShapeDtypeStruct((T, Z_COLS), F32),
        compiler_params=_cparams(("parallel",)),
        name="in_proj",
    )(x2, g, w)


def _compress_body(kc_ref, vc_ref, pek_ref, w1k_ref, w2k_ref, pev_ref, w1v_ref, w2v_ref, ko_ref, vo_ref):
    def one(c_ref, pe_ref, w1_ref, w2_ref, o_ref):
        ch = c_ref[...]
        a = jnp.dot((ch + pe_ref[0:1, :]).astype(BF16), w1_ref[0], preferred_element_type=F32)
        b = jnp.dot((ch + pe_ref[1:2, :]).astype(BF16), w1_ref[1], preferred_element_type=F32)
        n = b.shape[0]
        hid = jax.nn.gelu(a + pltpu.roll(b, n - 1, 0))
        o_ref[...] = jnp.dot(hid.astype(BF16), w2_ref[...], preferred_element_type=F32).astype(o_ref.dtype)

    one(kc_ref, pek_ref, w1k_ref, w2k_ref, ko_ref)
    one(vc_ref, pev_ref, w1v_ref, w2v_ref, vo_ref)


def _compress(kc2, vc2, pek, w1k, w2k, pev, w1v, w2v):
    B, n_ch, wide = kc2.shape
    hid2 = NSA_GROUPS * CMP_HIDDEN
    act = pl.BlockSpec((None, n_ch, wide), lambda b: (b, 0, 0))
    pe = pl.BlockSpec((2, wide), lambda b: (0, 0))
    w1 = pl.BlockSpec((2, wide, hid2), lambda b: (0, 0, 0))
    w2 = pl.BlockSpec((hid2, NSA_KV), lambda b: (0, 0))
    out = pl.BlockSpec((None, n_ch, NSA_KV), lambda b: (b, 0, 0))
    return pl.pallas_call(
        _compress_body,
        grid=(B,),
        in_specs=[act, act, pe, w1, w2, pe, w1, w2],
        out_specs=[out, out],
        out_shape=[jax.ShapeDtypeStruct((B, n_ch, NSA_KV), BF16)] * 2,
        compiler_params=_cparams(("parallel",)),
        name="nsa_compress",
    )(kc2, vc2, pek, w1k, w2k, pev, w1v, w2v)


SEL_TILE = 2 * Q_BLOCK
V_ROWS = NSA_DH + 16
NEAR_TILES = 3


def _nsa_body(q_ref, gn_ref, ks_ref, vs_ref, kw_ref, vw_ref, kcmp_ref, vcmpt_ref,
              et_ref, ovt_ref, bnear_ref, cfar_ref, bwin_ref, bcmp_ref, o_ref,
              kaug_scr, kw_scr, vst_ref, vwt_ref, part_scr, near_scr, sa_scr, sb_scr, pa_scr, pb_scr):
    c = pl.program_id(1)
    n_s = et_ref.shape[1]
    n_cp = kcmp_ref.shape[1]
    QB, DH, HPG = Q_BLOCK, NSA_DH, NSA_HPG
    s_len = ks_ref.shape[0]

    @pl.when(c == 0)
    def _():
        for g in range(NSA_GROUPS):
            kaug_scr[g, 0:s_len, 0:DH] = ks_ref[:, g * DH:(g + 1) * DH].astype(BF16)
            kaug_scr[g, 0:s_len, DH:DH + n_s] = et_ref[...]
            kaug_scr[g, s_len:s_len + SEL_TILE, 0:DH] = jnp.zeros((SEL_TILE, DH), BF16)
            kaug_scr[g, s_len:s_len + SEL_TILE, DH:DH + n_s] = jnp.ones((SEL_TILE, n_s), BF16)
            kw_scr[g, 0:WINDOW, :] = (lax.broadcasted_iota(jnp.int32, (WINDOW, LANE), 1) == DH).astype(BF16)
            kw_scr[g, WINDOW:WINDOW + s_len, 0:DH] = kw_ref[:, g * DH:(g + 1) * DH].astype(BF16)
            kw_scr[g, WINDOW:WINDOW + s_len, DH:LANE] = jnp.zeros((s_len, LANE - DH), BF16)
            vwt_ref[g, :, 0:WINDOW] = jnp.zeros((V_ROWS, WINDOW), BF16)

        def transpose_values(t, _):
            r0 = pl.multiple_of(t * QB, QB)
            for src, dst, pad in ((vs_ref, vst_ref, 0), (vw_ref, vwt_ref, WINDOW)):
                v_t = src[pl.ds(r0, QB), :].T.astype(BF16)
                for g in range(NSA_GROUPS):
                    dst[g, 0:DH, pl.ds(pad + r0, QB)] = v_t[g * DH:(g + 1) * DH]
            return 0

        lax.fori_loop(0, s_len // QB, transpose_values, 0)
        ones_row = (lax.broadcasted_iota(jnp.int32, (V_ROWS - DH, s_len), 0) == 0).astype(BF16)
        for g in range(NSA_GROUPS):
            vst_ref[g, DH:V_ROWS, :] = ones_row
            vwt_ref[g, DH:V_ROWS, WINDOW:WINDOW + s_len] = ones_row

    gates = jax.nn.sigmoid(gn_ref[...].T)
    q_t = [q_ref[:, i * LANE:(i + 1) * LANE].T for i in range(BRANCH_W // LANE)]

    blk = lax.broadcasted_iota(jnp.int32, (n_s, QB), 0)
    cur = (c * QB + lax.broadcasted_iota(jnp.int32, (n_s, QB), 1)) // SEL_BLOCK
    valid = blk <= cur
    forced = (blk == 0) | (blk == cur) | (blk == cur - 1)
    blk_f = blk.astype(F32)

    cmp_off = pl.multiple_of(n_cp - c * (QB // CMP_STRIDE), QB // CMP_STRIDE)
    n_far = jnp.maximum(c - 1, 0)
    n_w = WINDOW // QB + 1

    def normalised(acc):
        return acc[0:DH] * (1.0 / jnp.maximum(acc[DH:DH + 1], 1e-30))

    def gate_row(g, branch):
        return jnp.concatenate([gates[3 * (g * HPG + h) + branch:3 * (g * HPG + h) + branch + 1, :]
                                for h in range(HPG)], axis=1)

    near_tiles = ((n_far - 1, n_far % 2 == 1), (c - 1, c >= 1), (c, None))
    near_rows = [pl.multiple_of(t * QB if ok is None else jnp.where(ok, t * QB, s_len), QB) for t, ok in near_tiles]
    near_cols = [pl.multiple_of(jnp.maximum(t, 0) * QB, QB) for t, _ in near_tiles]
    win_flag = jnp.where(lax.broadcasted_iota(jnp.int32, (LANE - DH, HPG * QB), 0) == 0, NEG, 0.0).astype(BF16)
    w0 = pl.multiple_of(c * QB, QB)
    groups = range(NSA_GROUPS)

    qts = [jnp.concatenate([q_t[(g * HPG + h) * DH // LANE][((g * HPG + h) * DH) % LANE:((g * HPG + h) * DH) % LANE + DH]
                            for h in range(HPG)], axis=1).astype(BF16) for g in groups]
    s_cs = [jnp.dot(kcmp_ref[g], qts[g], preferred_element_type=F32) for g in groups]
    s_ws = [jnp.dot(kw_scr[g, pl.ds(w0, WINDOW + QB), :], jnp.concatenate([qts[g], win_flag], axis=0),
                    preferred_element_type=F32) for g in groups]

    qas, o_cs = [], []
    for g in groups:
        s_c = s_cs[g] + bcmp_ref[g, pl.ds(cmp_off, n_cp), :]
        e_c = jnp.exp2(s_c - jnp.maximum(jnp.max(s_c, axis=0, keepdims=True), 0.1 * NEG))
        p_c = e_c * (1.0 / jnp.maximum(jnp.sum(e_c, axis=0, keepdims=True), 1e-30))
        o_cs.append(jnp.dot(vcmpt_ref[g], p_c.astype(BF16), preferred_element_type=F32))

        p_sum = p_c[:, 0:QB] + p_c[:, QB:2 * QB] + p_c[:, 2 * QB:3 * QB] + p_c[:, 3 * QB:4 * QB]
        p_hi = p_sum.astype(BF16)
        p_lo = (p_sum - p_hi.astype(F32)).astype(BF16)
        imp = (jnp.dot(ovt_ref[...], p_hi, preferred_element_type=F32)
               + jnp.dot(ovt_ref[...], p_lo, preferred_element_type=F32))
        score = jnp.where(forced, imp + 1e4, jnp.where(valid, imp, -1e4))
        selneg = jnp.full((n_s, QB), NEG, F32)
        for _ in range(min(SEL_TOPK, n_s)):
            m = jnp.max(score, axis=0, keepdims=True)
            first = jnp.min(jnp.where(score == m, blk_f, float(n_s)), axis=0, keepdims=True)
            pick = blk_f == first
            selneg = jnp.where(pick, 0.0, selneg)
            score = jnp.where(pick, -jnp.inf, score)
        qas.append(jnp.concatenate([qts[g], jnp.concatenate([selneg.astype(BF16)] * HPG, axis=1)], axis=0))

    s_ns = [jnp.dot(jnp.concatenate([kaug_scr[g, pl.ds(r0, QB), :] for r0 in near_rows], axis=0), qas[g],
                    preferred_element_type=F32) for g in groups]

    for g in groups:
        s_w = s_ws[g] + bwin_ref[g]
        e_w = jnp.exp2(s_w - jnp.max(s_w, axis=0, keepdims=True))
        o_w = normalised(jnp.dot(vwt_ref[g, :, pl.ds(w0, WINDOW + QB)], e_w.astype(BF16),
                                 preferred_element_type=F32))
        part_scr[g] = gate_row(g, 0) * o_cs[g] + gate_row(g, 2) * o_w

    near_m = []
    for g in groups:
        s_n = s_ns[g] + bnear_ref[g]
        m_n = jnp.max(s_n, axis=0, keepdims=True)
        near_m.append(m_n)
        v_n = jnp.concatenate([vst_ref[g, :, pl.ds(r0, QB)] for r0 in near_cols], axis=1)
        near_scr[g] = jnp.dot(v_n, jnp.exp2(s_n - m_n).astype(BF16), preferred_element_type=F32)

    n_it = n_far // 2

    def scores(kt, g, dst):
        r0 = pl.multiple_of(jnp.where(kt < n_it, kt * SEL_TILE, s_len), SEL_TILE)
        dst[g] = jnp.dot(kaug_scr[g, pl.ds(r0, SEL_TILE), :], qas[g], preferred_element_type=F32)

    def values(kt, g, p_src):
        r0 = pl.multiple_of(jnp.clip(kt, 0, jnp.maximum(n_it - 1, 0)) * SEL_TILE, SEL_TILE)
        return jnp.dot(vst_ref[g, :, pl.ds(r0, SEL_TILE)], p_src[g], preferred_element_type=F32)

    def half(kt, carries, src, dst, p_src, p_dst):
        for g in groups:
            scores(kt + 1, g, dst)
        pv = [values(kt - 1, g, p_src) for g in groups]
        out = []
        for g in groups:
            m_i, acc, alpha_prev = carries[g]
            s = src[g]
            m_n = jnp.maximum(m_i, jnp.max(s, axis=0, keepdims=True))
            p_dst[g] = jnp.exp2(s - m_n).astype(BF16)
            out.append((m_n, alpha_prev * acc + pv[g], jnp.exp2(m_i - m_n)))
        return tuple(out)

    def far(j, carries):
        first = half(2 * j, carries, sa_scr, sb_scr, pb_scr, pa_scr)
        return half(2 * j + 1, first, sb_scr, sa_scr, pa_scr, pb_scr)

    for g in groups:
        pb_scr[g] = jnp.zeros((SEL_TILE, HPG * QB), BF16)
        scores(0, g, sa_scr)
    init = (jnp.full((1, HPG * QB), NEG, F32), jnp.zeros((V_ROWS, HPG * QB), F32), jnp.ones((1, HPG * QB), F32))
    n_trips = (n_it + 1) // 2
    piped = lax.fori_loop(0, n_trips, far, (init,) * NSA_GROUPS)
    carries = [(m_i, alpha_prev * acc + values(2 * n_trips - 1, g, pb_scr))
               for g, (m_i, acc, alpha_prev) in enumerate(piped)]

    for g in range(NSA_GROUPS):
        m_f, acc_f = carries[g]
        m_f = m_f + cfar_ref[g]
        m_s = jnp.maximum(m_f, near_m[g])
        acc_s = jnp.exp2(m_f - m_s) * acc_f + jnp.exp2(near_m[g] - m_s) * near_scr[g]
        o_g = part_scr[g] + gate_row(g, 1) * normalised(acc_s)
        for i in range(HPG * DH // LANE):
            pair = jnp.concatenate([o_g[:, (2 * i) * QB:(2 * i + 1) * QB], o_g[:, (2 * i + 1) * QB:(2 * i + 2) * QB]], axis=0)
            col = g * HPG * DH + i * LANE
            o_ref[:, col:col + LANE] = pair.T.astype(o_ref.dtype)


def _nsa_attention(z3, kcmp, vcmpt, tabs):
    B, S, _ = z3.shape
    et, ovt, bnear, cfar, bwin, bcmp = tabs

    def zcol(width, col):
        return pl.BlockSpec((None, Q_BLOCK, width), lambda b, c: (b, c, col // width))

    def zfull(col):
        return pl.BlockSpec((None, S, NSA_KV), lambda b, c: (b, 0, col // NSA_KV))

    def per_batch(a):
        nd = a.ndim - 1
        return pl.BlockSpec((None,) + a.shape[1:], lambda b, c: (b,) + (0,) * nd)

    def const(a):
        nd = a.ndim
        return pl.BlockSpec(a.shape, lambda b, c: (0,) * nd)

    tile = (NSA_GROUPS, SEL_TILE, NSA_HPG * Q_BLOCK)
    return pl.pallas_call(
        _nsa_body,
        grid=(B, S // Q_BLOCK),
        in_specs=[zcol(BRANCH_W, COL_QN), zcol(LANE, COL_GN),
                  zfull(COL_KC + 2 * NSA_KV), zfull(COL_KC + 3 * NSA_KV),
                  zfull(COL_KC + 4 * NSA_KV), zfull(COL_KC + 5 * NSA_KV),
                  per_batch(kcmp), per_batch(vcmpt),
                  const(et), const(ovt), const(bnear), const(cfar), const(bwin), const(bcmp)],
        out_specs=pl.BlockSpec((None, Q_BLOCK, BRANCH_W), lambda b, c: (b, c, 0)),
        out_shape=jax.ShapeDtypeStruct((B, S, BRANCH_W), BF16),
        scratch_shapes=[pltpu.VMEM((NSA_GROUPS, S + SEL_TILE, NSA_DH + et.shape[1]), BF16),
                        pltpu.VMEM((NSA_GROUPS, WINDOW + S, LANE), BF16),
                        pltpu.VMEM((NSA_GROUPS, V_ROWS, S), BF16), pltpu.VMEM((NSA_GROUPS, V_ROWS, WINDOW + S), BF16),
                        pltpu.VMEM((NSA_GROUPS, NSA_DH, NSA_HPG * Q_BLOCK), F32),
                        pltpu.VMEM((NSA_GROUPS, V_ROWS, NSA_HPG * Q_BLOCK), F32),
                        pltpu.VMEM(tile, F32), pltpu.VMEM(tile, F32), pltpu.VMEM(tile, BF16), pltpu.VMEM(tile, BF16)],
        compiler_params=_cparams(("parallel", "arbitrary")),
        name="nsa_attention",
    )(z3, z3, z3, z3, z3, z3, kcmp, vcmpt, et, ovt, bnear, cfar, bwin, bcmp)


def _retention_body(gch_ref, q_ref, k_ref, v_ref, g_ref, cos_ref, sin_ref, decay_ref, hm_ref, xi_ref,
                    zeta_ref, gn_ref, o_ref, state_scr):
    @pl.when(pl.program_id(1) == 0)
    def _():
        state_scr[...] = jnp.zeros_like(state_scr)

    half = RET_HEADS * RET_DK // 2
    cos, sin = cos_ref[...], sin_ref[...]

    def rot(x):
        x1, x2 = x[:, :half], x[:, half:]
        return jnp.concatenate([x1 * cos - x2 * sin, x1 * sin + x2 * cos], axis=1)

    qr = rot(q_ref[...])
    kr = rot(k_ref[...])
    kb = kr.astype(BF16)
    nt = (((1,), (1,)), ((), ()))
    tn = (((0,), (0,)), ((), ()))
    for h in range(RET_HEADS):
        cols = slice(h * RET_DV, (h + 1) * RET_DV)
        vh = v_ref[:, cols].astype(BF16)
        inner = lax.dot_general((qr * hm_ref[h]).astype(BF16), kb, nt, preferred_element_type=F32) * decay_ref[h]
        o = jnp.dot(inner.astype(BF16), vh, preferred_element_type=F32)
        state = state_scr[h]
        o = o + jnp.dot((qr * xi_ref[h]).astype(BF16), state.astype(BF16), preferred_element_type=F32)
        kv = lax.dot_general((kr * zeta_ref[h]).astype(BF16), vh, tn, preferred_element_type=F32)
        state_scr[h] = gch_ref[h] * state + kv
        mu = jnp.mean(o, axis=-1, keepdims=True)
        d = o - mu
        var = jnp.mean(d * d, axis=-1, keepdims=True)
        y = d * lax.rsqrt(var + EPS) * gn_ref[:, cols]
        gate = g_ref[:, cols]
        o_ref[:, cols] = (gate * jax.nn.sigmoid(gate) * y).astype(o_ref.dtype)


def _retention(z3, tabs, gn):
    B, S, _ = z3.shape
    C = RET_CHUNK
    gch, cos, sin, decay, hm, xi, zeta = tabs
    qk_w = RET_HEADS * RET_DK

    def zcol(width, col):
        return pl.BlockSpec((None, C, width), lambda b, n: (b, n, col // width))

    def const(a):
        nd = a.ndim
        return pl.BlockSpec(a.shape, lambda b, n: (0,) * nd)

    pos = pl.BlockSpec((C, qk_w // 2), lambda b, n: (n, 0))
    return pl.pallas_call(
        _retention_body,
        grid=(B, S // C),
        in_specs=[pl.BlockSpec(memory_space=pltpu.SMEM),
                  zcol(qk_w, COL_QR), zcol(qk_w, COL_KR), zcol(BRANCH_W, COL_VR), zcol(BRANCH_W, COL_GR),
                  pos, pos, const(decay), const(hm), const(xi), const(zeta), const(gn)],
        out_specs=pl.BlockSpec((None, C, BRANCH_W), lambda b, n: (b, n, 0)),
        out_shape=jax.ShapeDtypeStruct((B, S, BRANCH_W), BF16),
        scratch_shapes=[pltpu.VMEM((RET_HEADS, qk_w, RET_DV), F32)],
        compiler_params=_cparams(("parallel", "arbitrary")),
        name="retention",
    )(gch, z3, z3, z3, z3, cos, sin, decay, hm, xi, zeta, gn)


CONV_HALO = 32
SUBLANES = 8
CONV_ROWS = 64


def _conv_body(a_ref, b_ref, w_ref, cb_ref, lg_ref, lb_ref, o_ref, u_scr):
    tile = a_ref.shape[0]
    rows = CONV_HALO + tile

    @pl.when(pl.program_id(1) == 0)
    def _():
        u_scr[0, 0:CONV_HALO, :] = jnp.zeros((CONV_HALO, u_scr.shape[2]), F32)

    @pl.when(pl.program_id(1) > 0)
    def _():
        u_scr[0, 0:CONV_HALO, :] = u_scr[0, tile:rows, :]

    u_scr[0, CONV_HALO:rows, :] = a_ref[...] * jax.nn.sigmoid(b_ref[...])
    for r in range(1, SUBLANES):
        u_scr[r, 0:rows - SUBLANES, :] = u_scr[0, r:rows - SUBLANES + r, :]
    lead = CONV_HALO - (CONV_WIDTH - 1)

    for r0 in range(0, tile, CONV_ROWS):
        acc = jnp.zeros((CONV_ROWS, u_scr.shape[2]), F32) + cb_ref[...]
        for k in range(CONV_WIDTH):
            start = r0 + (lead + k) // SUBLANES * SUBLANES
            acc = acc + w_ref[k:k + 1, :] * u_scr[(lead + k) % SUBLANES, start:start + CONV_ROWS, :]
        mu = jnp.mean(acc, axis=-1, keepdims=True)
        d = acc - mu
        var = jnp.mean(d * d, axis=-1, keepdims=True)
        y = d * lax.rsqrt(var + EPS) * lg_ref[...] + lb_ref[...]
        o_ref[r0:r0 + CONV_ROWS, :] = (y * jax.nn.sigmoid(y)).astype(o_ref.dtype)


def _conv_module(z3, w, cb, lg, lb):
    B, S, _ = z3.shape
    tile = min(256, S)

    def zcol(col):
        return pl.BlockSpec((None, tile, BRANCH_W), lambda b, i: (b, i, col // BRANCH_W))

    def const(a):
        return pl.BlockSpec(a.shape, lambda b, i: (0, 0))

    return pl.pallas_call(
        _conv_body,
        grid=(B, S // tile),
        in_specs=[zcol(COL_CA), zcol(COL_CB), const(w), const(cb), const(lg), const(lb)],
        out_specs=pl.BlockSpec((None, tile, BRANCH_W), lambda b, i: (b, i, 0)),
        out_shape=jax.ShapeDtypeStruct((B, S, BRANCH_W), BF16),
        scratch_shapes=[pltpu.VMEM((SUBLANES, CONV_HALO + tile, BRANCH_W), F32)],
        compiler_params=_cparams(("parallel", "arbitrary")),
        name="conv_module",
    )(z3, z3, w, cb, lg, lb)


def _merge_body(x_ref, g_ref, wg_ref, a_ref, r_ref, c_ref, wb_ref, wo_ref, o_ref):
    x = x_ref[...]
    h = _rmsnorm_bf16(x, g_ref[...])
    merged = None
    for i, br in enumerate((a_ref, r_ref, c_ref)):
        gate = jnp.dot(h, wg_ref[:, i * D_MODEL:(i + 1) * D_MODEL], preferred_element_type=F32)
        proj = jnp.dot(br[...], wb_ref[i], preferred_element_type=F32)
        term = jax.nn.sigmoid(gate) * proj
        merged = term if merged is None else merged + term
    o_ref[...] = x + jnp.dot(merged.astype(BF16), wo_ref[...], preferred_element_type=F32)


def _merge(x2, g, wg, o_nsa, o_ret, o_conv, wb, wo):
    T = x2.shape[0]
    tm = 512
    br = pl.BlockSpec((tm, BRANCH_W), lambda i: (i, 0))
    return pl.pallas_call(
        _merge_body,
        grid=(T // tm,),
        in_specs=[pl.BlockSpec((tm, D_MODEL), lambda i: (i, 0)),
                  pl.BlockSpec((1, D_MODEL), lambda i: (0, 0)),
                  pl.BlockSpec(wg.shape, lambda i: (0, 0)),
                  br, br, br,
                  pl.BlockSpec(wb.shape, lambda i: (0, 0, 0)),
                  pl.BlockSpec(wo.shape, lambda i: (0, 0))],
        out_specs=pl.BlockSpec((tm, D_MODEL), lambda i: (i, 0)),
        out_shape=jax.ShapeDtypeStruct((T, D_MODEL), F32),
        compiler_params=_cparams(("parallel",)),
        name="merge_out_proj",
    )(x2, g, wg, o_nsa, o_ret, o_conv, wb, wo)


FF_CHUNK = 1024


def _ffn_body(x_ref, g_ref, w1_ref, w2_ref, gf_ref, o_ref, *, final_norm):
    x = x_ref[...]
    ms = jnp.mean(x * x, axis=-1, keepdims=True)
    h = (x * lax.rsqrt(ms + EPS) * g_ref[...]).astype(BF16)
    acc = x
    for j in range(D_FF // FF_CHUNK):
        cols = slice(j * FF_CHUNK, (j + 1) * FF_CHUNK)
        a = jnp.maximum(jnp.dot(h, w1_ref[:, cols], preferred_element_type=F32), 0.0)
        acc = acc + jnp.dot((a * a).astype(BF16), w2_ref[cols, :], preferred_element_type=F32)
    if final_norm:
        ms = jnp.mean(acc * acc, axis=-1, keepdims=True)
        acc = acc * lax.rsqrt(ms + EPS) * gf_ref[...]
    o_ref[...] = acc


def _ffn(x2, g, w1, w2, gf, final_norm):
    T = x2.shape[0]
    tm = 512
    vec = pl.BlockSpec((1, D_MODEL), lambda i: (0, 0))
    return pl.pallas_call(
        functools.partial(_ffn_body, final_norm=final_norm),
        grid=(T // tm,),
        in_specs=[pl.BlockSpec((tm, D_MODEL), lambda i: (i, 0)), vec,
                  pl.BlockSpec(w1.shape, lambda i: (0, 0)),
                  pl.BlockSpec(w2.shape, lambda i: (0, 0)), vec],
        out_specs=pl.BlockSpec((tm, D_MODEL), lambda i: (i, 0)),
        out_shape=jax.ShapeDtypeStruct((T, D_MODEL), F32),
        compiler_params=_cparams(("parallel",)),
        name="ffn",
    )(x2, g, w1, w2, gf)


def _rel_bucket(dist):
    n = jnp.maximum(dist, 0)
    max_exact = REL_BUCKETS // 2
    nf = jnp.maximum(n, 1).astype(F32)
    large = max_exact + (jnp.log(nf / max_exact) / math.log(REL_MAX_DIST / max_exact)
                         * (REL_BUCKETS - max_exact)).astype(jnp.int32)
    large = jnp.minimum(large, REL_BUCKETS - 1)
    return jnp.where(n < max_exact, n, large)


def _nsa_tables(rel_table, S):
    n_s = S // SEL_BLOCK
    n_cp = S // CMP_STRIDE
    QB, G, HPG = Q_BLOCK, NSA_GROUPS, NSA_HPG
    tab = rel_table.T.astype(F32) * LOG2E

    def bias(dist, ok=None):
        bucket = _rel_bucket(dist)[None]
        full = jnp.zeros((NSA_HEADS,) + dist.shape, F32)
        for b in range(REL_BUCKETS):
            full = jnp.where(bucket == b, tab[:, b][:, None, None], full)
        if ok is not None:
            full = jnp.where(ok[None], full, NEG)
        J = dist.shape[0]
        return jnp.transpose(full.reshape(G, HPG, J, QB), (0, 2, 1, 3)).reshape(G, J, HPG * QB)

    i = jnp.arange(QB)[None, :]
    et = (jnp.arange(S)[:, None] // SEL_BLOCK == jnp.arange(n_s)[None, :]).astype(BF16)
    cmp_start = jnp.arange(n_cp) * CMP_STRIDE
    sel_start = jnp.arange(n_s) * SEL_BLOCK
    ovt = ((cmp_start[None, :] <= sel_start[:, None] + SEL_BLOCK - 1)
           & (cmp_start[None, :] + CMP_BLOCK - 1 >= sel_start[:, None])
           & (jnp.arange(n_cp)[None, :] < n_cp - 1)).astype(BF16)
    d_near = i + (NEAR_TILES - 1) * QB - jnp.arange(NEAR_TILES * QB)[:, None]
    bnear = bias(d_near, d_near >= 0)
    cfar = bias(jnp.full((1, QB), S + REL_MAX_DIST, jnp.int32))
    d_win = i + WINDOW - jnp.arange(WINDOW + QB)[:, None]
    bwin = bias(d_win, (d_win >= 0) & (d_win < WINDOW))
    d_cmp = i - CMP_STRIDE * (jnp.arange(2 * n_cp)[:, None] - n_cp) - (CMP_BLOCK - 1)
    bcmp = bias(d_cmp, d_cmp >= 0)
    return et, ovt, bnear, cfar, bwin, bcmp


def _retention_tables(S):
    H, DK, C = RET_HEADS, RET_DK, RET_CHUNK
    half = DK // 2
    pos = jnp.arange(S, dtype=F32)
    inv = ROPE_BASE ** (-jnp.arange(half, dtype=F32) / half)
    ang = pos[:, None] * inv[None, :]
    cos = jnp.tile(jnp.cos(ang), (1, H))
    sin = jnp.tile(jnp.sin(ang), (1, H))
    log_g = jnp.log1p(-jnp.exp2(-5.0 - jnp.arange(H, dtype=F32)))
    ar = jnp.arange(C)
    diff = ar[:, None] - ar[None, :]
    decay = jnp.where(diff >= 0, jnp.exp(log_g[:, None, None] * jnp.maximum(diff, 0).astype(F32)), 0.0)
    zeta = jnp.exp(log_g[:, None] * (C - 1 - ar).astype(F32))
    xi = jnp.exp(log_g[:, None] * (ar + 1).astype(F32))
    gch = jnp.exp(log_g * C)
    lane_head = (jnp.arange(H * DK) % (H * half)) // half
    hm = (lane_head[None, :] == jnp.arange(H)[:, None]).astype(F32)
    return (gch, cos, sin, decay, hm[:, None, :], xi[:, :, None] * hm[:, None, :],
            zeta[:, :, None] * hm[:, None, :])


def _pack_w_in(w):
    o = np.cumsum([0, 512, 128, 128, 128, 128, 128, 128, 24, 256, 256, 512, 512, 512, 512, 3072])
    seg = lambda i: w[:, int(o[i]):int(o[i + 1])]

    def halves_major(a):
        a4 = a.reshape(a.shape[0], RET_HEADS, 2, RET_DK // 2)
        return jnp.transpose(a4, (0, 2, 1, 3)).reshape(a.shape[0], RET_HEADS * RET_DK)

    parts = [seg(0) * (NSA_DH ** -0.5 * LOG2E), seg(10), seg(11), seg(12), seg(13),
             halves_major(seg(8)), halves_major(seg(9)) * (RET_DK ** -0.5),
             seg(1), seg(2), seg(3), seg(4), seg(5), seg(6),
             seg(7), jnp.zeros((w.shape[0], LANE - 3 * NSA_HEADS), w.dtype)]
    return jnp.concatenate(parts, axis=1).astype(BF16), seg(14).astype(BF16)


def _pack_compress(pe, w1, w2):
    G, DH, r = NSA_GROUPS, NSA_DH, CMP_BLOCK // CMP_STRIDE
    eye = jnp.eye(G, dtype=w1.dtype)
    pe2 = jnp.tile(pe.reshape(r, CMP_STRIDE, 1, DH), (1, 1, G, 1)).reshape(r, CMP_STRIDE * G * DH)
    w1r = w1.reshape(r, CMP_STRIDE, DH, CMP_HIDDEN)
    w1b = jnp.einsum('rldf,gk->rlgdkf', w1r, eye).reshape(r, CMP_STRIDE * G * DH, G * CMP_HIDDEN)
    w2b = jnp.einsum('fd,gk->gfkd', w2, eye).reshape(G * CMP_HIDDEN, G * DH)
    return pe2, w1b.astype(BF16), w2b.astype(BF16)


def kernel(x, rel_table, norm_mix, w_in, cmp_pe_k, cmp_w1_k, cmp_w2_k, cmp_pe_v, cmp_w1_v, cmp_w2_v, ret_gn, conv_w, conv_b, conv_ln_g, conv_ln_b, w_branch, w_out, norm_mlp, w_ff1, w_ff2, norm_final):
    B, S, D = x.shape
    depth = w_in.shape[0]
    T = B * S
    nsa_tabs = _nsa_tables(rel_table, S)
    ret_tabs = _retention_tables(S)
    x2 = x.reshape(T, D)
    for l in range(depth):
        w_main, w_gate = _pack_w_in(w_in[l])
        z2 = _in_proj(x2, norm_mix[l][None, :], w_main)
        z3 = z2.reshape(B, S, Z_COLS)
        n_ch = S // CMP_STRIDE
        kc2 = z3[:, :, COL_KC:COL_KC + NSA_KV].reshape(B, n_ch, CMP_STRIDE * NSA_KV)
        vc2 = z3[:, :, COL_KC + NSA_KV:COL_KC + 2 * NSA_KV].reshape(B, n_ch, CMP_STRIDE * NSA_KV)
        kcmp, vcmp = _compress(kc2, vc2, *_pack_compress(cmp_pe_k[l], cmp_w1_k[l], cmp_w2_k[l]),
                               *_pack_compress(cmp_pe_v[l], cmp_w1_v[l], cmp_w2_v[l]))
        kcmp = jnp.transpose(kcmp.reshape(B, n_ch, NSA_GROUPS, NSA_DH), (0, 2, 1, 3))
        vcmpt = jnp.transpose(vcmp.reshape(B, n_ch, NSA_GROUPS, NSA_DH), (0, 2, 3, 1))

        o_nsa = _nsa_attention(z3, kcmp, vcmpt, nsa_tabs)
        o_ret = _retention(z3, ret_tabs, ret_gn[l][None, :])
        conv_w_pad = jnp.concatenate([conv_w[l], jnp.zeros((1, BRANCH_W), conv_w.dtype)], axis=0)
        o_conv = _conv_module(z3, conv_w_pad, conv_b[l][None, :], conv_ln_g[l][None, :], conv_ln_b[l][None, :])
        x2 = _merge(x2, norm_mix[l][None, :], w_gate, o_nsa.reshape(T, BRANCH_W), o_ret.reshape(T, BRANCH_W), o_conv.reshape(T, BRANCH_W),
                    w_branch[l].astype(BF16), w_out[l].astype(BF16))
        x2 = _ffn(x2, norm_mlp[l][None, :], w_ff1[l].astype(BF16), w_ff2[l].astype(BF16), norm_final[None, :],
                  final_norm=(l == depth - 1))
    return x2.reshape(B, S, D)
```

```python
import functools
import math

import jax
import jax.numpy as jnp
import numpy as np
from jax import lax
from jax.experimental import pallas as pl
from jax.experimental.pallas import tpu as pltpu

F32 = jnp.float32
BF16 = jnp.bfloat16

D_MODEL = 1024
BRANCH_W = 512
N_BRANCH = 3
NSA_HEADS = 8
NSA_GROUPS = 2
NSA_HPG = 4
NSA_DH = 64
NSA_KV = NSA_GROUPS * NSA_DH
CMP_BLOCK = 32
CMP_STRIDE = 16
CMP_HIDDEN = 128
SEL_BLOCK = 64
SEL_TOPK = 8
WINDOW = 512
Q_BLOCK = 128
RET_HEADS = 4
RET_DV = 128
RET_DK = 64
RET_CHUNK = 128
ROPE_BASE = 10000.0
CONV_WIDTH = 31
REL_BUCKETS = 32
REL_MAX_DIST = 128
D_FF = 4096
EPS = 1e-6
NEG = -1e30
LOG2E = 1.4426950408889634

LANE = 128
VMEM_LIMIT = 56 * 1024 * 1024

COL_QN = 0
COL_VR = 512
COL_GR = 1024
COL_CA = 1536
COL_CB = 2048
COL_QR = 2560
COL_KR = 2816
COL_KC = 3072
COL_GN = 3840
Z_COLS = 3968


def _cparams(sem):
    return pltpu.CompilerParams(dimension_semantics=sem, vmem_limit_bytes=VMEM_LIMIT)


def _rmsnorm_bf16(x, g):
    ms = jnp.mean(x * x, axis=-1, keepdims=True)
    return (x * lax.rsqrt(ms + EPS) * g).astype(BF16)


def _in_proj_body(x_ref, g_ref, w_ref, z_ref):
    z_ref[...] = jnp.dot(_rmsnorm_bf16(x_ref[...], g_ref[...]), w_ref[...], preferred_element_type=F32)


def _in_proj(x2, g, w):
    T = x2.shape[0]
    tm = 512
    return pl.pallas_call(
        _in_proj_body,
        grid=(T // tm,),
        in_specs=[pl.BlockSpec((tm, D_MODEL), lambda i: (i, 0)),
                  pl.BlockSpec((1, D_MODEL), lambda i: (0, 0)),
                  pl.BlockSpec((D_MODEL, Z_COLS), lambda i: (0, 0))],
        out_specs=pl.BlockSpec((tm, Z_COLS), lambda i: (i, 0)),
        out_shape=jax.ShapeDtypeStruct((T, Z_COLS), F32),
        compiler_params=_cparams(("parallel",)),
        name="in_proj",
    )(x2, g, w)


def _compress_body(kc_ref, vc_ref, pek_ref, w1k_ref, w2k_ref, pev_ref, w1v_ref, w2v_ref, ko_ref, vo_ref):
    def one(c_ref, pe_ref, w1_ref, w2_ref, o_ref):
        ch = c_ref[...]
        a = jnp.dot((ch + pe_ref[0:1, :]).astype(BF16), w1_ref[0], preferred_element_type=F32)
        b = jnp.dot((ch + pe_ref[1:2, :]).astype(BF16), w1_ref[1], preferred_element_type=F32)
        n = b.shape[0]
        hid = jax.nn.gelu(a + pltpu.roll(b, n - 1, 0))
        o_ref[...] = jnp.dot(hid.astype(BF16), w2_ref[...], preferred_element_type=F32).astype(o_ref.dtype)

    one(kc_ref, pek_ref, w1k_ref, w2k_ref, ko_ref)
    one(vc_ref, pev_ref, w1v_ref, w2v_ref, vo_ref)


def _compress(kc2, vc2, pek, w1k, w2k, pev, w1v, w2v):
    B, n_ch, wide = kc2.shape
    hid2 = NSA_GROUPS * CMP_HIDDEN
    act = pl.BlockSpec((None, n_ch, wide), lambda b: (b, 0, 0))
    pe = pl.BlockSpec((2, wide), lambda b: (0, 0))
    w1 = pl.BlockSpec((2, wide, hid2), lambda b: (0, 0, 0))
    w2 = pl.BlockSpec((hid2, NSA_KV), lambda b: (0, 0))
    out = pl.BlockSpec((None, n_ch, NSA_KV), lambda b: (b, 0, 0))
    return pl.pallas_call(
        _compress_body,
        grid=(B,),
        in_specs=[act, act, pe, w1, w2, pe, w1, w2],
        out_specs=[out, out],
        out_shape=[jax.ShapeDtypeStruct((B, n_ch, NSA_KV), BF16)] * 2,
        compiler_params=_cparams(("parallel",)),
        name="nsa_compress",
    )(kc2, vc2, pek, w1k, w2k, pev, w1v, w2v)


SEL_TILE = 2 * Q_BLOCK
V_ROWS = NSA_DH + 16
NEAR_TILES = 3


def _nsa_body(q_ref, gn_ref, ks_ref, vs_ref, kw_ref, vw_ref, kcmp_ref, vcmpt_ref,
              et_ref, ovt_ref, bnear_ref, cfar_ref, bwin_ref, bcmp_ref, o_ref,
              kaug_scr, kw_scr, vst_ref, vwt_ref, part_scr, near_scr, sa_scr, sb_scr, pa_scr, pb_scr):
    c = pl.program_id(1)
    n_s = et_ref.shape[1]
    n_cp = kcmp_ref.shape[1]
    QB, DH, HPG = Q_BLOCK, NSA_DH, NSA_HPG
    s_len = ks_ref.shape[0]

    @pl.when(c == 0)
    def _():
        for g in range(NSA_GROUPS):
            kaug_scr[g, 0:s_len, 0:DH] = ks_ref[:, g * DH:(g + 1) * DH].astype(BF16)
            kaug_scr[g, 0:s_len, DH:DH + n_s] = et_ref[...]
            kaug_scr[g, s_len:s_len + SEL_TILE, 0:DH] = jnp.zeros((SEL_TILE, DH), BF16)
            kaug_scr[g, s_len:s_len + SEL_TILE, DH:DH + n_s] = jnp.ones((SEL_TILE, n_s), BF16)
            kw_scr[g, 0:WINDOW, :] = (lax.broadcasted_iota(jnp.int32, (WINDOW, LANE), 1) == DH).astype(BF16)
            kw_scr[g, WINDOW:WINDOW + s_len, 0:DH] = kw_ref[:, g * DH:(g + 1) * DH].astype(BF16)
            kw_scr[g, WINDOW:WINDOW + s_len, DH:LANE] = jnp.zeros((s_len, LANE - DH), BF16)
            vwt_ref[g, :, 0:WINDOW] = jnp.zeros((V_ROWS, WINDOW), BF16)

        def transpose_values(t, _):
            r0 = pl.multiple_of(t * QB, QB)
            for src, dst, pad in ((vs_ref, vst_ref, 0), (vw_ref, vwt_ref, WINDOW)):
                v_t = src[pl.ds(r0, QB), :].T.astype(BF16)
                for g in range(NSA_GROUPS):
                    dst[g, 0:DH, pl.ds(pad + r0, QB)] = v_t[g * DH:(g + 1) * DH]
            return 0

        lax.fori_loop(0, s_len // QB, transpose_values, 0)
        ones_row = (lax.broadcasted_iota(jnp.int32, (V_ROWS - DH, s_len), 0) == 0).astype(BF16)
        for g in range(NSA_GROUPS):
            vst_ref[g, DH:V_ROWS, :] = ones_row
            vwt_ref[g, DH:V_ROWS, WINDOW:WINDOW + s_len] = ones_row

    gates = jax.nn.sigmoid(gn_ref[...].T)
    q_t = [q_ref[:, i * LANE:(i + 1) * LANE].T for i in range(BRANCH_W // LANE)]

    blk = lax.broadcasted_iota(jnp.int32, (n_s, QB), 0)
    cur = (c * QB + lax.broadcasted_iota(jnp.int32, (n_s, QB), 1)) // SEL_BLOCK
    valid = blk <= cur
    forced = (blk == 0) | (blk == cur) | (blk == cur - 1)
    blk_f = blk.astype(F32)

    cmp_off = pl.multiple_of(n_cp - c * (QB // CMP_STRIDE), QB // CMP_STRIDE)
    n_far = jnp.maximum(c - 1, 0)
    n_w = WINDOW // QB + 1

    def normalised(acc):
        return acc[0:DH] * (1.0 / jnp.maximum(acc[DH:DH + 1], 1e-30))

    def gate_row(g, branch):
        return jnp.concatenate([gates[3 * (g * HPG + h) + branch:3 * (g * HPG + h) + branch + 1, :]
                                for h in range(HPG)], axis=1)

    near_tiles = ((n_far - 1, n_far % 2 == 1), (c - 1, c >= 1), (c, None))
    near_rows = [pl.multiple_of(t * QB if ok is None else jnp.where(ok, t * QB, s_len), QB) for t, ok in near_tiles]
    near_cols = [pl.multiple_of(jnp.maximum(t, 0) * QB, QB) for t, _ in near_tiles]
    win_flag = jnp.where(lax.broadcasted_iota(jnp.int32, (LANE - DH, HPG * QB), 0) == 0, NEG, 0.0).astype(BF16)
    w0 = pl.multiple_of(c * QB, QB)
    groups = range(NSA_GROUPS)

    qts = [jnp.concatenate([q_t[(g * HPG + h) * DH // LANE][((g * HPG + h) * DH) % LANE:((g * HPG + h) * DH) % LANE + DH]
                            for h in range(HPG)], axis=1).astype(BF16) for g in groups]
    s_cs = [jnp.dot(kcmp_ref[g], qts[g], preferred_element_type=F32) for g in groups]
    s_ws = [jnp.dot(kw_scr[g, pl.ds(w0, WINDOW + QB), :], jnp.concatenate([qts[g], win_flag], axis=0),
                    preferred_element_type=F32) for g in groups]

    qas, o_cs = [], []
    for g in groups:
        s_c = s_cs[g] + bcmp_ref[g, pl.ds(cmp_off, n_cp), :]
        e_c = jnp.exp2(s_c - jnp.maximum(jnp.max(s_c, axis=0, keepdims=True), 0.1 * NEG))
        p_c = e_c * (1.0 / jnp.maximum(jnp.sum(e_c, axis=0, keepdims=True), 1e-30))
        o_cs.append(jnp.dot(vcmpt_ref[g], p_c.astype(BF16), preferred_element_type=F32))

        p_sum = p_c[:, 0:QB] + p_c[:, QB:2 * QB] + p_c[:, 2 * QB:3 * QB] + p_c[:, 3 * QB:4 * QB]
        p_hi = p_sum.astype(BF16)
        p_lo = (p_sum - p_hi.astype(F32)).astype(BF16)
        imp = (jnp.dot(ovt_ref[...], p_hi, preferred_element_type=F32)
               + jnp.dot(ovt_ref[...], p_lo, preferred_element_type=F32))
        score = jnp.where(forced, imp + 1e4, jnp.where(valid, imp, -1e4))
        selneg = jnp.full((n_s, QB), NEG, F32)
        for _ in range(min(SEL_TOPK, n_s)):
            m = jnp.max(score, axis=0, keepdims=True)
            first = jnp.min(jnp.where(score == m, blk_f, float(n_s)), axis=0, keepdims=True)
            pick = blk_f == first
            selneg = jnp.where(pick, 0.0, selneg)
            score = jnp.where(pick, -jnp.inf, score)
        qas.append(jnp.concatenate([qts[g], jnp.concatenate([selneg.astype(BF16)] * HPG, axis=1)], axis=0))

    s_ns = [jnp.dot(jnp.concatenate([kaug_scr[g, pl.ds(r0, QB), :] for r0 in near_rows], axis=0), qas[g],
                    preferred_element_type=F32) for g in groups]

    for g in groups:
        s_w = s_ws[g] + bwin_ref[g]
        e_w = jnp.exp2(s_w - jnp.max(s_w, axis=0, keepdims=True))
        o_w = normalised(jnp.dot(vwt_ref[g, :, pl.ds(w0, WINDOW + QB)], e_w.astype(BF16),
                                 preferred_element_type=F32))
        part_scr[g] = gate_row(g, 0) * o_cs[g] + gate_row(g, 2) * o_w

    near_m = []
    for g in groups:
        s_n = s_ns[g] + bnear_ref[g]
        m_n = jnp.max(s_n, axis=0, keepdims=True)
        near_m.append(m_n)
        v_n = jnp.concatenate([vst_ref[g, :, pl.ds(r0, QB)] for r0 in near_cols], axis=1)
        near_scr[g] = jnp.dot(v_n, jnp.exp2(s_n - m_n).astype(BF16), preferred_element_type=F32)

    n_it = n_far // 2

    def scores(kt, g, dst):
        r0 = pl.multiple_of(jnp.where(kt < n_it, kt * SEL_TILE, s_len), SEL_TILE)
        dst[g] = jnp.dot(kaug_scr[g, pl.ds(r0, SEL_TILE), :], qas[g], preferred_element_type=F32)

    def values(kt, g, p_src):
        r0 = pl.multiple_of(jnp.clip(kt, 0, jnp.maximum(n_it - 1, 0)) * SEL_TILE, SEL_TILE)
        return jnp.dot(vst_ref[g, :, pl.ds(r0, SEL_TILE)], p_src[g], preferred_element_type=F32)

    def half(kt, carries, src, dst, p_src, p_dst):
        for g in groups:
            scores(kt + 1, g, dst)
        pv = [values(kt - 1, g, p_src) for g in groups]
        out = []
        for g in groups:
            m_i, acc, alpha_prev = carries[g]
            s = src[g]
            m_n = jnp.maximum(m_i, jnp.max(s, axis=0, keepdims=True))
            p_dst[g] = jnp.exp2(s - m_n).astype(BF16)
            out.append((m_n, alpha_prev * acc + pv[g], jnp.exp2(m_i - m_n)))
        return tuple(out)

    def far(j, carries):
        first = half(2 * j, carries, sa_scr, sb_scr, pb_scr, pa_scr)
        return half(2 * j + 1, first, sb_scr, sa_scr, pa_scr, pb_scr)

    for g in groups:
        pb_scr[g] = jnp.zeros((SEL_TILE, HPG * QB), BF16)
        scores(0, g, sa_scr)
    init = (jnp.full((1, HPG * QB), NEG, F32), jnp.zeros((V_ROWS, HPG * QB), F32), jnp.ones((1, HPG * QB), F32))
    n_trips = (n_it + 1) // 2
    piped = lax.fori_loop(0, n_trips, far, (init,) * NSA_GROUPS)
    carries = [(m_i, alpha_prev * acc + values(2 * n_trips - 1, g, pb_scr))
               for g, (m_i, acc, alpha_prev) in enumerate(piped)]

    for g in range(NSA_GROUPS):
        m_f, acc_f = carries[g]
        m_f = m_f + cfar_ref[g]
        m_s = jnp.maximum(m_f, near_m[g])
        acc_s = jnp.exp2(m_f - m_s) * acc_f + jnp.exp2(near_m[g] - m_s) * near_scr[g]
        o_g = part_scr[g] + gate_row(g, 1) * normalised(acc_s)
        for i in range(HPG * DH // LANE):
            pair = jnp.concatenate([o_g[:, (2 * i) * QB:(2 * i + 1) * QB], o_g[:, (2 * i + 1) * QB:(2 * i + 2) * QB]], axis=0)
            col = g * HPG * DH + i * LANE
            o_ref[:, col:col + LANE] = pair.T.astype(o_ref.dtype)


def _nsa_attention(z3, kcmp, vcmpt, tabs):
    B, S, _ = z3.shape
    et, ovt, bnear, cfar, bwin, bcmp = tabs

    def zcol(width, col):
        return pl.BlockSpec((None, Q_BLOCK, width), lambda b, c: (b, c, col // width))

    def zfull(col):
        return pl.BlockSpec((None, S, NSA_KV), lambda b, c: (b, 0, col // NSA_KV))

    def per_batch(a):
        nd = a.ndim - 1
        return pl.BlockSpec((None,) + a.shape[1:], lambda b, c: (b,) + (0,) * nd)

    def const(a):
        nd = a.ndim
        return pl.BlockSpec(a.shape, lambda b, c: (0,) * nd)

    tile = (NSA_GROUPS, SEL_TILE, NSA_HPG * Q_BLOCK)
    return pl.pallas_call(
        _nsa_body,
        grid=(B, S // Q_BLOCK),
        in_specs=[zcol(BRANCH_W, COL_QN), zcol(LANE, COL_GN),
                  zfull(COL_KC + 2 * NSA_KV), zfull(COL_KC + 3 * NSA_KV),
                  zfull(COL_KC + 4 * NSA_KV), zfull(COL_KC + 5 * NSA_KV),
                  per_batch(kcmp), per_batch(vcmpt),
                  const(et), const(ovt), const(bnear), const(cfar), const(bwin), const(bcmp)],
        out_specs=pl.BlockSpec((None, Q_BLOCK, BRANCH_W), lambda b, c: (b, c, 0)),
        out_shape=jax.ShapeDtypeStruct((B, S, BRANCH_W), BF16),
        scratch_shapes=[pltpu.VMEM((NSA_GROUPS, S + SEL_TILE, NSA_DH + et.shape[1]), BF16),
                        pltpu.VMEM((NSA_GROUPS, WINDOW + S, LANE), BF16),
                        pltpu.VMEM((NSA_GROUPS, V_ROWS, S), BF16), pltpu.VMEM((NSA_GROUPS, V_ROWS, WINDOW + S), BF16),
                        pltpu.VMEM((NSA_GROUPS, NSA_DH, NSA_HPG * Q_BLOCK), F32),
                        pltpu.VMEM((NSA_GROUPS, V_ROWS, NSA_HPG * Q_BLOCK), F32),
                        pltpu.VMEM(tile, F32), pltpu.VMEM(tile, F32), pltpu.VMEM(tile, BF16), pltpu.VMEM(tile, BF16)],
        compiler_params=_cparams(("parallel", "arbitrary")),
        name="nsa_attention",
    )(z3, z3, z3, z3, z3, z3, kcmp, vcmpt, et, ovt, bnear, cfar, bwin, bcmp)


RET_SEQS = 4


def _retention_body(gch_ref, q_ref, k_ref, v_ref, g_ref, cos_ref, sin_ref, decay_ref, hm_ref, xi_ref,
                    zeta_ref, gn_ref, o_ref, state_scr):
    @pl.when(pl.program_id(1) == 0)
    def _():
        state_scr[...] = jnp.zeros_like(state_scr)

    half = RET_HEADS * RET_DK // 2
    cos, sin = cos_ref[...], sin_ref[...]

    def rot(x):
        x1, x2 = x[:, :half], x[:, half:]
        return jnp.concatenate([x1 * cos - x2 * sin, x1 * sin + x2 * cos], axis=1)

    nt = (((1,), (1,)), ((), ()))
    tn = (((0,), (0,)), ((), ()))
    pairs = [(r, h) for r in range(RET_SEQS) for h in range(RET_HEADS)]
    qr = [rot(q_ref[r]) for r in range(RET_SEQS)]
    kr = [rot(k_ref[r]) for r in range(RET_SEQS)]
    kb = [x.astype(BF16) for x in kr]
    vh = {(r, h): v_ref[r, :, h * RET_DV:(h + 1) * RET_DV].astype(BF16) for r, h in pairs}
    inner = {(r, h): lax.dot_general((qr[r] * hm_ref[h]).astype(BF16), kb[r], nt, preferred_element_type=F32)
             for r, h in pairs}
    state = {(r, h): state_scr[r, h] for r, h in pairs}
    cross = {(r, h): jnp.dot((qr[r] * xi_ref[h]).astype(BF16), state[r, h].astype(BF16),
                             preferred_element_type=F32) for r, h in pairs}
    kv = {(r, h): lax.dot_general((kr[r] * zeta_ref[h]).astype(BF16), vh[r, h], tn, preferred_element_type=F32)
          for r, h in pairs}
    for r, h in pairs:
        state_scr[r, h] = gch_ref[h] * state[r, h] + kv[r, h]
    for r, h in pairs:
        cols = slice(h * RET_DV, (h + 1) * RET_DV)
        o = jnp.dot((inner[r, h] * decay_ref[h]).astype(BF16), vh[r, h], preferred_element_type=F32) + cross[r, h]
        mu = jnp.mean(o, axis=-1, keepdims=True)
        d = o - mu
        var = jnp.mean(d * d, axis=-1, keepdims=True)
        y = d * lax.rsqrt(var + EPS) * gn_ref[:, cols]
        gate = g_ref[r, :, cols]
        o_ref[r, :, cols] = (gate * jax.nn.sigmoid(gate) * y).astype(o_ref.dtype)


def _retention(z3, tabs, gn):
    B, S, _ = z3.shape
    C = RET_CHUNK
    gch, cos, sin, decay, hm, xi, zeta = tabs
    qk_w = RET_HEADS * RET_DK

    def zcol(width, col):
        return pl.BlockSpec((RET_SEQS, C, width), lambda b, n: (b, n, col // width))

    def const(a):
        nd = a.ndim
        return pl.BlockSpec(a.shape, lambda b, n: (0,) * nd)

    pos = pl.BlockSpec((C, qk_w // 2), lambda b, n: (n, 0))
    return pl.pallas_call(
        _retention_body,
        grid=(B // RET_SEQS, S // C),
        in_specs=[pl.BlockSpec(memory_space=pltpu.SMEM),
                  zcol(qk_w, COL_QR), zcol(qk_w, COL_KR), zcol(BRANCH_W, COL_VR), zcol(BRANCH_W, COL_GR),
                  pos, pos, const(decay), const(hm), const(xi), const(zeta), const(gn)],
        out_specs=pl.BlockSpec((RET_SEQS, C, BRANCH_W), lambda b, n: (b, n, 0)),
        out_shape=jax.ShapeDtypeStruct((B, S, BRANCH_W), BF16),
        scratch_shapes=[pltpu.VMEM((RET_SEQS, RET_HEADS, qk_w, RET_DV), F32)],
        compiler_params=_cparams(("parallel", "arbitrary")),
        name="retention",
    )(gch, z3, z3, z3, z3, cos, sin, decay, hm, xi, zeta, gn)


CONV_HALO = 32
SUBLANES = 8
CONV_ROWS = 64


def _conv_body(a_ref, b_ref, w_ref, cb_ref, lg_ref, lb_ref, o_ref, u_scr):
    tile = a_ref.shape[0]
    rows = CONV_HALO + tile

    @pl.when(pl.program_id(1) == 0)
    def _():
        u_scr[0, 0:CONV_HALO, :] = jnp.zeros((CONV_HALO, u_scr.shape[2]), F32)

    @pl.when(pl.program_id(1) > 0)
    def _():
        u_scr[0, 0:CONV_HALO, :] = u_scr[0, tile:rows, :]

    u_scr[0, CONV_HALO:rows, :] = a_ref[...] * jax.nn.sigmoid(b_ref[...])
    for r in range(1, SUBLANES):
        u_scr[r, 0:rows - SUBLANES, :] = u_scr[0, r:rows - SUBLANES + r, :]
    lead = CONV_HALO - (CONV_WIDTH - 1)

    for r0 in range(0, tile, CONV_ROWS):
        acc = jnp.zeros((CONV_ROWS, u_scr.shape[2]), F32) + cb_ref[...]
        for k in range(CONV_WIDTH):
            start = r0 + (lead + k) // SUBLANES * SUBLANES
            acc = acc + w_ref[k:k + 1, :] * u_scr[(lead + k) % SUBLANES, start:start + CONV_ROWS, :]
        mu = jnp.mean(acc, axis=-1, keepdims=True)
        d = acc - mu
        var = jnp.mean(d * d, axis=-1, keepdims=True)
        y = d * lax.rsqrt(var + EPS) * lg_ref[...] + lb_ref[...]
        o_ref[r0:r0 + CONV_ROWS, :] = (y * jax.nn.sigmoid(y)).astype(o_ref.dtype)


def _conv_module(z3, w, cb, lg, lb):
    B, S, _ = z3.shape
    tile = min(256, S)

    def zcol(col):
        return pl.BlockSpec((None, tile, BRANCH_W), lambda b, i: (b, i, col // BRANCH_W))

    def const(a):
        return pl.BlockSpec(a.shape, lambda b, i: (0, 0))

    return pl.pallas_call(
        _conv_body,
        grid=(B, S // tile),
        in_specs=[zcol(COL_CA), zcol(COL_CB), const(w), const(cb), const(lg), const(lb)],
        out_specs=pl.BlockSpec((None, tile, BRANCH_W), lambda b, i: (b, i, 0)),
        out_shape=jax.ShapeDtypeStruct((B, S, BRANCH_W), BF16),
        scratch_shapes=[pltpu.VMEM((SUBLANES, CONV_HALO + tile, BRANCH_W), F32)],
        compiler_params=_cparams(("parallel", "arbitrary")),
        name="conv_module",
    )(z3, z3, w, cb, lg, lb)


def _merge_body(x_ref, g_ref, wg_ref, a_ref, r_ref, c_ref, wb_ref, wo_ref, o_ref):
    x = x_ref[...]
    h = _rmsnorm_bf16(x, g_ref[...])
    merged = None
    for i, br in enumerate((a_ref, r_ref, c_ref)):
        gate = jnp.dot(h, wg_ref[:, i * D_MODEL:(i + 1) * D_MODEL], preferred_element_type=F32)
        proj = jnp.dot(br[...], wb_ref[i], preferred_element_type=F32)
        term = jax.nn.sigmoid(gate) * proj
        merged = term if merged is None else merged + term
    o_ref[...] = x + jnp.dot(merged.astype(BF16), wo_ref[...], preferred_element_type=F32)


def _merge(x2, g, wg, o_nsa, o_ret, o_conv, wb, wo):
    T = x2.shape[0]
    tm = 512
    br = pl.BlockSpec((tm, BRANCH_W), lambda i: (i, 0))
    return pl.pallas_call(
        _merge_body,
        grid=(T // tm,),
        in_specs=[pl.BlockSpec((tm, D_MODEL), lambda i: (i, 0)),
                  pl.BlockSpec((1, D_MODEL), lambda i: (0, 0)),
                  pl.BlockSpec(wg.shape, lambda i: (0, 0)),
                  br, br, br,
                  pl.BlockSpec(wb.shape, lambda i: (0, 0, 0)),
                  pl.BlockSpec(wo.shape, lambda i: (0, 0))],
        out_specs=pl.BlockSpec((tm, D_MODEL), lambda i: (i, 0)),
        out_shape=jax.ShapeDtypeStruct((T, D_MODEL), F32),
        compiler_params=_cparams(("parallel",)),
        name="merge_out_proj",
    )(x2, g, wg, o_nsa, o_ret, o_conv, wb, wo)


FF_CHUNK = 1024


def _ffn_body(x_ref, g_ref, w1_ref, w2_ref, gf_ref, o_ref, *, final_norm):
    x = x_ref[...]
    ms = jnp.mean(x * x, axis=-1, keepdims=True)
    h = (x * lax.rsqrt(ms + EPS) * g_ref[...]).astype(BF16)
    acc = x
    for j in range(D_FF // FF_CHUNK):
        cols = slice(j * FF_CHUNK, (j + 1) * FF_CHUNK)
        a = jnp.maximum(jnp.dot(h, w1_ref[:, cols], preferred_element_type=F32), 0.0)
        acc = acc + jnp.dot((a * a).astype(BF16), w2_ref[cols, :], preferred_element_type=F32)
    if final_norm:
        ms = jnp.mean(acc * acc, axis=-1, keepdims=True)
        acc = acc * lax.rsqrt(ms + EPS) * gf_ref[...]
    o_ref[...] = acc


def _ffn(x2, g, w1, w2, gf, final_norm):
    T = x2.shape[0]
    tm = 512
    vec = pl.BlockSpec((1, D_MODEL), lambda i: (0, 0))
    return pl.pallas_call(
        functools.partial(_ffn_body, final_norm=final_norm),
        grid=(T // tm,),
        in_specs=[pl.BlockSpec((tm, D_MODEL), lambda i: (i, 0)), vec,
                  pl.BlockSpec(w1.shape, lambda i: (0, 0)),
                  pl.BlockSpec(w2.shape, lambda i: (0, 0)), vec],
        out_specs=pl.BlockSpec((tm, D_MODEL), lambda i: (i, 0)),
        out_shape=jax.ShapeDtypeStruct((T, D_MODEL), F32),
        compiler_params=_cparams(("parallel",)),
        name="ffn",
    )(x2, g, w1, w2, gf)


def _rel_bucket(dist):
    n = jnp.maximum(dist, 0)
    max_exact = REL_BUCKETS // 2
    nf = jnp.maximum(n, 1).astype(F32)
    large = max_exact + (jnp.log(nf / max_exact) / math.log(REL_MAX_DIST / max_exact)
                         * (REL_BUCKETS - max_exact)).astype(jnp.int32)
    large = jnp.minimum(large, REL_BUCKETS - 1)
    return jnp.where(n < max_exact, n, large)


def _nsa_tables(rel_table, S):
    n_s = S // SEL_BLOCK
    n_cp = S // CMP_STRIDE
    QB, G, HPG = Q_BLOCK, NSA_GROUPS, NSA_HPG
    tab = rel_table.T.astype(F32) * LOG2E

    def bias(dist, ok=None):
        bucket = _rel_bucket(dist)[None]
        full = jnp.zeros((NSA_HEADS,) + dist.shape, F32)
        for b in range(REL_BUCKETS):
            full = jnp.where(bucket == b, tab[:, b][:, None, None], full)
        if ok is not None:
            full = jnp.where(ok[None], full, NEG)
        J = dist.shape[0]
        return jnp.transpose(full.reshape(G, HPG, J, QB), (0, 2, 1, 3)).reshape(G, J, HPG * QB)

    i = jnp.arange(QB)[None, :]
    et = (jnp.arange(S)[:, None] // SEL_BLOCK == jnp.arange(n_s)[None, :]).astype(BF16)
    cmp_start = jnp.arange(n_cp) * CMP_STRIDE
    sel_start = jnp.arange(n_s) * SEL_BLOCK
    ovt = ((cmp_start[None, :] <= sel_start[:, None] + SEL_BLOCK - 1)
           & (cmp_start[None, :] + CMP_BLOCK - 1 >= sel_start[:, None])
           & (jnp.arange(n_cp)[None, :] < n_cp - 1)).astype(BF16)
    d_near = i + (NEAR_TILES - 1) * QB - jnp.arange(NEAR_TILES * QB)[:, None]
    bnear = bias(d_near, d_near >= 0)
    cfar = bias(jnp.full((1, QB), S + REL_MAX_DIST, jnp.int32))
    d_win = i + WINDOW - jnp.arange(WINDOW + QB)[:, None]
    bwin = bias(d_win, (d_win >= 0) & (d_win < WINDOW))
    d_cmp = i - CMP_STRIDE * (jnp.arange(2 * n_cp)[:, None] - n_cp) - (CMP_BLOCK - 1)
    bcmp = bias(d_cmp, d_cmp >= 0)
    return et, ovt, bnear, cfar, bwin, bcmp


def _retention_tables(S):
    H, DK, C = RET_HEADS, RET_DK, RET_CHUNK
    half = DK // 2
    pos = jnp.arange(S, dtype=F32)
    inv = ROPE_BASE ** (-jnp.arange(half, dtype=F32) / half)
    ang = pos[:, None] * inv[None, :]
    cos = jnp.tile(jnp.cos(ang), (1, H))
    sin = jnp.tile(jnp.sin(ang), (1, H))
    log_g = jnp.log1p(-jnp.exp2(-5.0 - jnp.arange(H, dtype=F32)))
    ar = jnp.arange(C)
    diff = ar[:, None] - ar[None, :]
    decay = jnp.where(diff >= 0, jnp.exp(log_g[:, None, None] * jnp.maximum(diff, 0).astype(F32)), 0.0)
    zeta = jnp.exp(log_g[:, None] * (C - 1 - ar).astype(F32))
    xi = jnp.exp(log_g[:, None] * (ar + 1).astype(F32))
    gch = jnp.exp(log_g * C)
    lane_head = (jnp.arange(H * DK) % (H * half)) // half
    hm = (lane_head[None, :] == jnp.arange(H)[:, None]).astype(F32)
    return (gch, cos, sin, decay, hm[:, None, :], xi[:, :, None] * hm[:, None, :],
            zeta[:, :, None] * hm[:, None, :])


def _pack_w_in(w):
    o = np.cumsum([0, 512, 128, 128, 128, 128, 128, 128, 24, 256, 256, 512, 512, 512, 512, 3072])
    seg = lambda i: w[:, int(o[i]):int(o[i + 1])]

    def halves_major(a):
        a4 = a.reshape(a.shape[0], RET_HEADS, 2, RET_DK // 2)
        return jnp.transpose(a4, (0, 2, 1, 3)).reshape(a.shape[0], RET_HEADS * RET_DK)

    parts = [seg(0) * (NSA_DH ** -0.5 * LOG2E), seg(10), seg(11), seg(12), seg(13),
             halves_major(seg(8)), halves_major(seg(9)) * (RET_DK ** -0.5),
             seg(1), seg(2), seg(3), seg(4), seg(5), seg(6),
             seg(7), jnp.zeros((w.shape[0], LANE - 3 * NSA_HEADS), w.dtype)]
    return jnp.concatenate(parts, axis=1).astype(BF16), seg(14).astype(BF16)


def _pack_compress(pe, w1, w2):
    G, DH, r = NSA_GROUPS, NSA_DH, CMP_BLOCK // CMP_STRIDE
    eye = jnp.eye(G, dtype=w1.dtype)
    pe2 = jnp.tile(pe.reshape(r, CMP_STRIDE, 1, DH), (1, 1, G, 1)).reshape(r, CMP_STRIDE * G * DH)
    w1r = w1.reshape(r, CMP_STRIDE, DH, CMP_HIDDEN)
    w1b = jnp.einsum('rldf,gk->rlgdkf', w1r, eye).reshape(r, CMP_STRIDE * G * DH, G * CMP_HIDDEN)
    w2b = jnp.einsum('fd,gk->gfkd', w2, eye).reshape(G * CMP_HIDDEN, G * DH)
    return pe2, w1b.astype(BF16), w2b.astype(BF16)


def kernel(x, rel_table, norm_mix, w_in, cmp_pe_k, cmp_w1_k, cmp_w2_k, cmp_pe_v, cmp_w1_v, cmp_w2_v, ret_gn, conv_w, conv_b, conv_ln_g, conv_ln_b, w_branch, w_out, norm_mlp, w_ff1, w_ff2, norm_final):
    B, S, D = x.shape
    depth = w_in.shape[0]
    T = B * S
    nsa_tabs = _nsa_tables(rel_table, S)
    ret_tabs = _retention_tables(S)
    x2 = x.reshape(T, D)
    for l in range(depth):
        w_main, w_gate = _pack_w_in(w_in[l])
        z2 = _in_proj(x2, norm_mix[l][None, :], w_main)
        z3 = z2.reshape(B, S, Z_COLS)
        n_ch = S // CMP_STRIDE
        kc2 = z3[:, :, COL_KC:COL_KC + NSA_KV].reshape(B, n_ch, CMP_STRIDE * NSA_KV)
        vc2 = z3[:, :, COL_KC + NSA_KV:COL_KC + 2 * NSA_KV].reshape(B, n_ch, CMP_STRIDE * NSA_KV)
        kcmp, vcmp = _compress(kc2, vc2, *_pack_compress(cmp_pe_k[l], cmp_w1_k[l], cmp_w2_k[l]),
                               *_pack_compress(cmp_pe_v[l], cmp_w1_v[l], cmp_w2_v[l]))
        kcmp = jnp.transpose(kcmp.reshape(B, n_ch, NSA_GROUPS, NSA_DH), (0, 2, 1, 3))
        vcmpt = jnp.transpose(vcmp.reshape(B, n_ch, NSA_GROUPS, NSA_DH), (0, 2, 3, 1))

        o_nsa = _nsa_attention(z3, kcmp, vcmpt, nsa_tabs)
        o_ret = _retention(z3, ret_tabs, ret_gn[l][None, :])
        conv_w_pad = jnp.concatenate([conv_w[l], jnp.zeros((1, BRANCH_W), conv_w.dtype)], axis=0)
        o_conv = _conv_module(z3, conv_w_pad, conv_b[l][None, :], conv_ln_g[l][None, :], conv_ln_b[l][None, :])
        x2 = _merge(x2, norm_mix[l][None, :], w_gate, o_nsa.reshape(T, BRANCH_W), o_ret.reshape(T, BRANCH_W), o_conv.reshape(T, BRANCH_W),
                    w_branch[l].astype(BF16), w_out[l].astype(BF16))
        x2 = _ffn(x2, norm_mlp[l][None, :], w_ff1[l].astype(BF16), w_ff2[l].astype(BF16), norm_final[None, :],
                  final_norm=(l == depth - 1))
    return x2.reshape(B, S, D)
```

```python
import functools
import math

import jax
import jax.numpy as jnp
import numpy as np
from jax import lax
from jax.experimental import pallas as pl
from jax.experimental.pallas import tpu as pltpu

F32 = jnp.float32
BF16 = jnp.bfloat16

D_MODEL = 1024
BRANCH_W = 512
N_BRANCH = 3
NSA_HEADS = 8
NSA_GROUPS = 2
NSA_HPG = 4
NSA_DH = 64
NSA_KV = NSA_GROUPS * NSA_DH
CMP_BLOCK = 32
CMP_STRIDE = 16
CMP_HIDDEN = 128
SEL_BLOCK = 64
SEL_TOPK = 8
WINDOW = 512
Q_BLOCK = 128
RET_HEADS = 4
RET_DV = 128
RET_DK = 64
RET_CHUNK = 128
ROPE_BASE = 10000.0
CONV_WIDTH = 31
REL_BUCKETS = 32
REL_MAX_DIST = 128
D_FF = 4096
EPS = 1e-6
NEG = -1e30
LOG2E = 1.4426950408889634

LANE = 128
VMEM_LIMIT = 56 * 1024 * 1024

COL_QN = 0
COL_VR = 512
COL_GR = 1024
COL_CA = 1536
COL_CB = 2048
COL_QR = 2560
COL_KR = 2816
COL_KS = 3072
COL_GN = 3584
Z_COLS = 3712
W_COLS = Z_COLS + 2 * NSA_KV


def _cparams(sem):
    return pltpu.CompilerParams(dimension_semantics=sem, vmem_limit_bytes=VMEM_LIMIT)


def _rmsnorm_bf16(x, g):
    ms = jnp.mean(x * x, axis=-1, keepdims=True)
    return (x * lax.rsqrt(ms + EPS) * g).astype(BF16)


def _in_proj_body(x_ref, g_ref, w_ref, z_ref, kc_ref, vc_ref):
    acc = jnp.dot(_rmsnorm_bf16(x_ref[...], g_ref[...]), w_ref[...], preferred_element_type=F32)
    z_ref[...] = acc[:, 0:Z_COLS]
    kc_ref[...] = acc[:, Z_COLS:Z_COLS + NSA_KV]
    vc_ref[...] = acc[:, Z_COLS + NSA_KV:W_COLS]


def _in_proj(x2, g, w):
    T = x2.shape[0]
    tm = 512
    return pl.pallas_call(
        _in_proj_body,
        grid=(T // tm,),
        in_specs=[pl.BlockSpec((tm, D_MODEL), lambda i: (i, 0)),
                  pl.BlockSpec((1, D_MODEL), lambda i: (0, 0)),
                  pl.BlockSpec((D_MODEL, W_COLS), lambda i: (0, 0))],
        out_specs=[pl.BlockSpec((tm, Z_COLS), lambda i: (i, 0)), pl.BlockSpec((tm, NSA_KV), lambda i: (i, 0)),
                   pl.BlockSpec((tm, NSA_KV), lambda i: (i, 0))],
        out_shape=[jax.ShapeDtypeStruct((T, Z_COLS), F32), jax.ShapeDtypeStruct((T, NSA_KV), F32),
                   jax.ShapeDtypeStruct((T, NSA_KV), F32)],
        compiler_params=_cparams(("parallel",)),
        name="in_proj",
    )(x2, g, w)


def _compress_body(kc_ref, vc_ref, pek_ref, w1k_ref, w2k_ref, pev_ref, w1v_ref, w2v_ref, ko_ref, vo_ref):
    def one(c_ref, pe_ref, w1_ref, w2_ref, o_ref):
        ch = c_ref[...]
        a = jnp.dot((ch + pe_ref[0:1, :]).astype(BF16), w1_ref[0], preferred_element_type=F32)
        b = jnp.dot((ch + pe_ref[1:2, :]).astype(BF16), w1_ref[1], preferred_element_type=F32)
        n = b.shape[0]
        hid = jax.nn.gelu(a + pltpu.roll(b, n - 1, 0))
        o_ref[...] = jnp.dot(hid.astype(BF16), w2_ref[...], preferred_element_type=F32).astype(o_ref.dtype)

    one(kc_ref, pek_ref, w1k_ref, w2k_ref, ko_ref)
    one(vc_ref, pev_ref, w1v_ref, w2v_ref, vo_ref)


def _compress(kc2, vc2, pek, w1k, w2k, pev, w1v, w2v):
    B, n_ch, wide = kc2.shape
    hid2 = NSA_GROUPS * CMP_HIDDEN
    act = pl.BlockSpec((None, n_ch, wide), lambda b: (b, 0, 0))
    pe = pl.BlockSpec((2, wide), lambda b: (0, 0))
    w1 = pl.BlockSpec((2, wide, hid2), lambda b: (0, 0, 0))
    w2 = pl.BlockSpec((hid2, NSA_KV), lambda b: (0, 0))
    out = pl.BlockSpec((None, n_ch, NSA_KV), lambda b: (b, 0, 0))
    return pl.pallas_call(
        _compress_body,
        grid=(B,),
        in_specs=[act, act, pe, w1, w2, pe, w1, w2],
        out_specs=[out, out],
        out_shape=[jax.ShapeDtypeStruct((B, n_ch, NSA_KV), BF16)] * 2,
        compiler_params=_cparams(("parallel",)),
        name="nsa_compress",
    )(kc2, vc2, pek, w1k, w2k, pev, w1v, w2v)


SEL_TILE = 2 * Q_BLOCK
V_ROWS = NSA_DH + 16
NEAR_TILES = 3


def _nsa_body(q_ref, gn_ref, ks_ref, vs_ref, kw_ref, vw_ref, kcmp_ref, vcmpt_ref,
              et_ref, ovt_ref, bnear_ref, cfar_ref, bwin_ref, bcmp_ref, o_ref,
              kaug_scr, kw_scr, vst_ref, vwt_ref, part_scr, near_scr, sa_scr, sb_scr, pa_scr, pb_scr):
    c = pl.program_id(1)
    n_s = et_ref.shape[1]
    n_cp = kcmp_ref.shape[1]
    QB, DH, HPG = Q_BLOCK, NSA_DH, NSA_HPG
    s_len = ks_ref.shape[0]

    @pl.when(c == 0)
    def _():
        for g in range(NSA_GROUPS):
            kaug_scr[g, 0:s_len, 0:DH] = ks_ref[:, g * DH:(g + 1) * DH].astype(BF16)
            kaug_scr[g, 0:s_len, DH:DH + n_s] = et_ref[...]
            kaug_scr[g, s_len:s_len + SEL_TILE, 0:DH] = jnp.zeros((SEL_TILE, DH), BF16)
            kaug_scr[g, s_len:s_len + SEL_TILE, DH:DH + n_s] = jnp.ones((SEL_TILE, n_s), BF16)
            kw_scr[g, 0:WINDOW, :] = (lax.broadcasted_iota(jnp.int32, (WINDOW, LANE), 1) == DH).astype(BF16)
            kw_scr[g, WINDOW:WINDOW + s_len, 0:DH] = kw_ref[:, g * DH:(g + 1) * DH].astype(BF16)
            kw_scr[g, WINDOW:WINDOW + s_len, DH:LANE] = jnp.zeros((s_len, LANE - DH), BF16)
            vwt_ref[g, :, 0:WINDOW] = jnp.zeros((V_ROWS, WINDOW), BF16)

        def transpose_values(t, _):
            r0 = pl.multiple_of(t * QB, QB)
            for src, dst, pad in ((vs_ref, vst_ref, 0), (vw_ref, vwt_ref, WINDOW)):
                v_t = src[pl.ds(r0, QB), :].T.astype(BF16)
                for g in range(NSA_GROUPS):
                    dst[g, 0:DH, pl.ds(pad + r0, QB)] = v_t[g * DH:(g + 1) * DH]
            return 0

        lax.fori_loop(0, s_len // QB, transpose_values, 0)
        ones_row = (lax.broadcasted_iota(jnp.int32, (V_ROWS - DH, s_len), 0) == 0).astype(BF16)
        for g in range(NSA_GROUPS):
            vst_ref[g, DH:V_ROWS, :] = ones_row
            vwt_ref[g, DH:V_ROWS, WINDOW:WINDOW + s_len] = ones_row

    gates = jax.nn.sigmoid(gn_ref[...].T)
    q_t = [q_ref[:, i * LANE:(i + 1) * LANE].T for i in range(BRANCH_W // LANE)]

    blk = lax.broadcasted_iota(jnp.int32, (n_s, QB), 0)
    cur = (c * QB + lax.broadcasted_iota(jnp.int32, (n_s, QB), 1)) // SEL_BLOCK
    valid = blk <= cur
    forced = (blk == 0) | (blk == cur) | (blk == cur - 1)
    blk_f = blk.astype(F32)

    cmp_off = pl.multiple_of(n_cp - c * (QB // CMP_STRIDE), QB // CMP_STRIDE)
    n_far = jnp.maximum(c - 1, 0)
    n_w = WINDOW // QB + 1

    def normalised(acc):
        return acc[0:DH] * (1.0 / jnp.maximum(acc[DH:DH + 1], 1e-30))

    def gate_row(g, branch):
        return jnp.concatenate([gates[3 * (g * HPG + h) + branch:3 * (g * HPG + h) + branch + 1, :]
                                for h in range(HPG)], axis=1)

    near_tiles = ((n_far - 1, n_far % 2 == 1), (c - 1, c >= 1), (c, None))
    near_rows = [pl.multiple_of(t * QB if ok is None else jnp.where(ok, t * QB, s_len), QB) for t, ok in near_tiles]
    near_cols = [pl.multiple_of(jnp.maximum(t, 0) * QB, QB) for t, _ in near_tiles]
    win_flag = jnp.where(lax.broadcasted_iota(jnp.int32, (LANE - DH, HPG * QB), 0) == 0, NEG, 0.0).astype(BF16)
    w0 = pl.multiple_of(c * QB, QB)
    groups = range(NSA_GROUPS)

    qts = [jnp.concatenate([q_t[(g * HPG + h) * DH // LANE][((g * HPG + h) * DH) % LANE:((g * HPG + h) * DH) % LANE + DH]
                            for h in range(HPG)], axis=1).astype(BF16) for g in groups]
    s_cs = [jnp.dot(kcmp_ref[g], qts[g], preferred_element_type=F32) for g in groups]
    s_ws = [jnp.dot(kw_scr[g, pl.ds(w0, WINDOW + QB), :], jnp.concatenate([qts[g], win_flag], axis=0),
                    preferred_element_type=F32) for g in groups]

    qas, o_cs = [], []
    for g in groups:
        s_c = s_cs[g] + bcmp_ref[g, pl.ds(cmp_off, n_cp), :]
        e_c = jnp.exp2(s_c - jnp.maximum(jnp.max(s_c, axis=0, keepdims=True), 0.1 * NEG))
        p_c = e_c * (1.0 / jnp.maximum(jnp.sum(e_c, axis=0, keepdims=True), 1e-30))
        o_cs.append(jnp.dot(vcmpt_ref[g], p_c.astype(BF16), preferred_element_type=F32))

        p_sum = p_c[:, 0:QB] + p_c[:, QB:2 * QB] + p_c[:, 2 * QB:3 * QB] + p_c[:, 3 * QB:4 * QB]
        p_hi = p_sum.astype(BF16)
        p_lo = (p_sum - p_hi.astype(F32)).astype(BF16)
        imp = (jnp.dot(ovt_ref[...], p_hi, preferred_element_type=F32)
               + jnp.dot(ovt_ref[...], p_lo, preferred_element_type=F32))
        score = jnp.where(forced, imp + 1e4, jnp.where(valid, imp, -1e4))
        selneg = jnp.full((n_s, QB), NEG, F32)
        for _ in range(min(SEL_TOPK, n_s)):
            m = jnp.max(score, axis=0, keepdims=True)
            first = jnp.min(jnp.where(score == m, blk_f, float(n_s)), axis=0, keepdims=True)
            pick = blk_f == first
            selneg = jnp.where(pick, 0.0, selneg)
            score = jnp.where(pick, -jnp.inf, score)
        qas.append(jnp.concatenate([qts[g], jnp.concatenate([selneg.astype(BF16)] * HPG, axis=1)], axis=0))

    s_ns = [jnp.dot(jnp.concatenate([kaug_scr[g, pl.ds(r0, QB), :] for r0 in near_rows], axis=0), qas[g],
                    preferred_element_type=F32) for g in groups]

    for g in groups:
        s_w = s_ws[g] + bwin_ref[g]
        e_w = jnp.exp2(s_w - jnp.max(s_w, axis=0, keepdims=True))
        o_w = normalised(jnp.dot(vwt_ref[g, :, pl.ds(w0, WINDOW + QB)], e_w.astype(BF16),
                                 preferred_element_type=F32))
        part_scr[g] = gate_row(g, 0) * o_cs[g] + gate_row(g, 2) * o_w

    near_m = []
    for g in groups:
        s_n = s_ns[g] + bnear_ref[g]
        m_n = jnp.max(s_n, axis=0, keepdims=True)
        near_m.append(m_n)
        v_n = jnp.concatenate([vst_ref[g, :, pl.ds(r0, QB)] for r0 in near_cols], axis=1)
        near_scr[g] = jnp.dot(v_n, jnp.exp2(s_n - m_n).astype(BF16), preferred_element_type=F32)

    n_it = n_far // 2

    def scores(kt, g, dst):
        r0 = pl.multiple_of(jnp.where(kt < n_it, kt * SEL_TILE, s_len), SEL_TILE)
        dst[g] = jnp.dot(kaug_scr[g, pl.ds(r0, SEL_TILE), :], qas[g], preferred_element_type=F32)

    def values(kt, g, p_src):
        r0 = pl.multiple_of(jnp.clip(kt, 0, jnp.maximum(n_it - 1, 0)) * SEL_TILE, SEL_TILE)
        return jnp.dot(vst_ref[g, :, pl.ds(r0, SEL_TILE)], p_src[g], preferred_element_type=F32)

    def half(kt, carries, src, dst, p_src, p_dst):
        for g in groups:
            scores(kt + 1, g, dst)
        pv = [values(kt - 1, g, p_src) for g in groups]
        out = []
        for g in groups:
            m_i, acc, alpha_prev = carries[g]
            s = src[g]
            m_n = jnp.maximum(m_i, jnp.max(s, axis=0, keepdims=True))
            p_dst[g] = jnp.exp2(s - m_n).astype(BF16)
            out.append((m_n, alpha_prev * acc + pv[g], jnp.exp2(m_i - m_n)))
        return tuple(out)

    def far(j, carries):
        first = half(2 * j, carries, sa_scr, sb_scr, pb_scr, pa_scr)
        return half(2 * j + 1, first, sb_scr, sa_scr, pa_scr, pb_scr)

    for g in groups:
        pb_scr[g] = jnp.zeros((SEL_TILE, HPG * QB), BF16)
        scores(0, g, sa_scr)
    init = (jnp.full((1, HPG * QB), NEG, F32), jnp.zeros((V_ROWS, HPG * QB), F32), jnp.ones((1, HPG * QB), F32))
    n_trips = (n_it + 1) // 2
    piped = lax.fori_loop(0, n_trips, far, (init,) * NSA_GROUPS)
    carries = [(m_i, alpha_prev * acc + values(2 * n_trips - 1, g, pb_scr))
               for g, (m_i, acc, alpha_prev) in enumerate(piped)]

    for g in range(NSA_GROUPS):
        m_f, acc_f = carries[g]
        m_f = m_f + cfar_ref[g]
        m_s = jnp.maximum(m_f, near_m[g])
        acc_s = jnp.exp2(m_f - m_s) * acc_f + jnp.exp2(near_m[g] - m_s) * near_scr[g]
        o_g = part_scr[g] + gate_row(g, 1) * normalised(acc_s)
        for i in range(HPG * DH // LANE):
            pair = jnp.concatenate([o_g[:, (2 * i) * QB:(2 * i + 1) * QB], o_g[:, (2 * i + 1) * QB:(2 * i + 2) * QB]], axis=0)
            col = g * HPG * DH + i * LANE
            o_ref[:, col:col + LANE] = pair.T.astype(o_ref.dtype)


def _nsa_attention(z3, kcmp, vcmpt, tabs):
    B, S, _ = z3.shape
    et, ovt, bnear, cfar, bwin, bcmp = tabs

    def zcol(width, col):
        return pl.BlockSpec((None, Q_BLOCK, width), lambda b, c: (b, c, col // width))

    def zfull(col):
        return pl.BlockSpec((None, S, NSA_KV), lambda b, c: (b, 0, col // NSA_KV))

    def per_batch(a):
        nd = a.ndim - 1
        return pl.BlockSpec((None,) + a.shape[1:], lambda b, c: (b,) + (0,) * nd)

    def const(a):
        nd = a.ndim
        return pl.BlockSpec(a.shape, lambda b, c: (0,) * nd)

    tile = (NSA_GROUPS, SEL_TILE, NSA_HPG * Q_BLOCK)
    return pl.pallas_call(
        _nsa_body,
        grid=(B, S // Q_BLOCK),
        in_specs=[zcol(BRANCH_W, COL_QN), zcol(LANE, COL_GN),
                  zfull(COL_KS), zfull(COL_KS + NSA_KV), zfull(COL_KS + 2 * NSA_KV), zfull(COL_KS + 3 * NSA_KV),
                  per_batch(kcmp), per_batch(vcmpt),
                  const(et), const(ovt), const(bnear), const(cfar), const(bwin), const(bcmp)],
        out_specs=pl.BlockSpec((None, Q_BLOCK, BRANCH_W), lambda b, c: (b, c, 0)),
        out_shape=jax.ShapeDtypeStruct((B, S, BRANCH_W), BF16),
        scratch_shapes=[pltpu.VMEM((NSA_GROUPS, S + SEL_TILE, NSA_DH + et.shape[1]), BF16),
                        pltpu.VMEM((NSA_GROUPS, WINDOW + S, LANE), BF16),
                        pltpu.VMEM((NSA_GROUPS, V_ROWS, S), BF16), pltpu.VMEM((NSA_GROUPS, V_ROWS, WINDOW + S), BF16),
                        pltpu.VMEM((NSA_GROUPS, NSA_DH, NSA_HPG * Q_BLOCK), F32),
                        pltpu.VMEM((NSA_GROUPS, V_ROWS, NSA_HPG * Q_BLOCK), F32),
                        pltpu.VMEM(tile, F32), pltpu.VMEM(tile, F32), pltpu.VMEM(tile, BF16), pltpu.VMEM(tile, BF16)],
        compiler_params=_cparams(("parallel", "arbitrary")),
        name="nsa_attention",
    )(z3, z3, z3, z3, z3, z3, kcmp, vcmpt, et, ovt, bnear, cfar, bwin, bcmp)


RET_SEQS = 4


def _retention_body(gch_ref, q_ref, k_ref, v_ref, g_ref, cos_ref, sin_ref, decay_ref, hm_ref, xi_ref,
                    zeta_ref, gn_ref, o_ref, state_scr):
    @pl.when(pl.program_id(1) == 0)
    def _():
        state_scr[...] = jnp.zeros_like(state_scr)

    half = RET_HEADS * RET_DK // 2
    cos, sin = cos_ref[...], sin_ref[...]

    def rot(x):
        x1, x2 = x[:, :half], x[:, half:]
        return jnp.concatenate([x1 * cos - x2 * sin, x1 * sin + x2 * cos], axis=1)

    nt = (((1,), (1,)), ((), ()))
    tn = (((0,), (0,)), ((), ()))
    pairs = [(r, h) for r in range(RET_SEQS) for h in range(RET_HEADS)]
    qr = [rot(q_ref[r]) for r in range(RET_SEQS)]
    kr = [rot(k_ref[r]) for r in range(RET_SEQS)]
    kb = [x.astype(BF16) for x in kr]
    vh = {(r, h): v_ref[r, :, h * RET_DV:(h + 1) * RET_DV].astype(BF16) for r, h in pairs}
    inner = {(r, h): lax.dot_general((qr[r] * hm_ref[h]).astype(BF16), kb[r], nt, preferred_element_type=F32)
             for r, h in pairs}
    state = {(r, h): state_scr[r, h] for r, h in pairs}
    cross = {(r, h): jnp.dot((qr[r] * xi_ref[h]).astype(BF16), state[r, h].astype(BF16),
                             preferred_element_type=F32) for r, h in pairs}
    kv = {(r, h): lax.dot_general((kr[r] * zeta_ref[h]).astype(BF16), vh[r, h], tn, preferred_element_type=F32)
          for r, h in pairs}
    for r, h in pairs:
        state_scr[r, h] = gch_ref[h] * state[r, h] + kv[r, h]
    for r, h in pairs:
        cols = slice(h * RET_DV, (h + 1) * RET_DV)
        o = jnp.dot((inner[r, h] * decay_ref[h]).astype(BF16), vh[r, h], preferred_element_type=F32) + cross[r, h]
        mu = jnp.mean(o, axis=-1, keepdims=True)
        d = o - mu
        var = jnp.mean(d * d, axis=-1, keepdims=True)
        y = d * lax.rsqrt(var + EPS) * gn_ref[:, cols]
        gate = g_ref[r, :, cols]
        o_ref[r, :, cols] = (gate * jax.nn.sigmoid(gate) * y).astype(o_ref.dtype)


def _retention(z3, tabs, gn):
    B, S, _ = z3.shape
    C = RET_CHUNK
    gch, cos, sin, decay, hm, xi, zeta = tabs
    qk_w = RET_HEADS * RET_DK

    def zcol(width, col):
        return pl.BlockSpec((RET_SEQS, C, width), lambda b, n: (b, n, col // width))

    def const(a):
        nd = a.ndim
        return pl.BlockSpec(a.shape, lambda b, n: (0,) * nd)

    pos = pl.BlockSpec((C, qk_w // 2), lambda b, n: (n, 0))
    return pl.pallas_call(
        _retention_body,
        grid=(B // RET_SEQS, S // C),
        in_specs=[pl.BlockSpec(memory_space=pltpu.SMEM),
                  zcol(qk_w, COL_QR), zcol(qk_w, COL_KR), zcol(BRANCH_W, COL_VR), zcol(BRANCH_W, COL_GR),
                  pos, pos, const(decay), const(hm), const(xi), const(zeta), const(gn)],
        out_specs=pl.BlockSpec((RET_SEQS, C, BRANCH_W), lambda b, n: (b, n, 0)),
        out_shape=jax.ShapeDtypeStruct((B, S, BRANCH_W), BF16),
        scratch_shapes=[pltpu.VMEM((RET_SEQS, RET_HEADS, qk_w, RET_DV), F32)],
        compiler_params=_cparams(("parallel", "arbitrary")),
        name="retention",
    )(gch, z3, z3, z3, z3, cos, sin, decay, hm, xi, zeta, gn)


CONV_HALO = 32
SUBLANES = 8
CONV_ROWS = 64


def _conv_body(a_ref, b_ref, w_ref, cb_ref, lg_ref, lb_ref, o_ref, u_scr):
    tile = a_ref.shape[0]
    rows = CONV_HALO + tile

    @pl.when(pl.program_id(1) == 0)
    def _():
        u_scr[0, 0:CONV_HALO, :] = jnp.zeros((CONV_HALO, u_scr.shape[2]), F32)

    @pl.when(pl.program_id(1) > 0)
    def _():
        u_scr[0, 0:CONV_HALO, :] = u_scr[0, tile:rows, :]

    u_scr[0, CONV_HALO:rows, :] = a_ref[...] * jax.nn.sigmoid(b_ref[...])
    for r in range(1, SUBLANES):
        u_scr[r, 0:rows - SUBLANES, :] = u_scr[0, r:rows - SUBLANES + r, :]
    lead = CONV_HALO - (CONV_WIDTH - 1)

    for r0 in range(0, tile, CONV_ROWS):
        acc = jnp.zeros((CONV_ROWS, u_scr.shape[2]), F32) + cb_ref[...]
        for k in range(CONV_WIDTH):
            start = r0 + (lead + k) // SUBLANES * SUBLANES
            acc = acc + w_ref[k:k + 1, :] * u_scr[(lead + k) % SUBLANES, start:start + CONV_ROWS, :]
        mu = jnp.mean(acc, axis=-1, keepdims=True)
        d = acc - mu
        var = jnp.mean(d * d, axis=-1, keepdims=True)
        y = d * lax.rsqrt(var + EPS) * lg_ref[...] + lb_ref[...]
        o_ref[r0:r0 + CONV_ROWS, :] = (y * jax.nn.sigmoid(y)).astype(o_ref.dtype)


def _conv_module(z3, w, cb, lg, lb):
    B, S, _ = z3.shape
    tile = min(512, S)

    def zcol(col):
        return pl.BlockSpec((None, tile, BRANCH_W), lambda b, i: (b, i, col // BRANCH_W))

    def const(a):
        return pl.BlockSpec(a.shape, lambda b, i: (0, 0))

    return pl.pallas_call(
        _conv_body,
        grid=(B, S // tile),
        in_specs=[zcol(COL_CA), zcol(COL_CB), const(w), const(cb), const(lg), const(lb)],
        out_specs=pl.BlockSpec((None, tile, BRANCH_W), lambda b, i: (b, i, 0)),
        out_shape=jax.ShapeDtypeStruct((B, S, BRANCH_W), BF16),
        scratch_shapes=[pltpu.VMEM((SUBLANES, CONV_HALO + tile, BRANCH_W), F32)],
        compiler_params=_cparams(("parallel", "arbitrary")),
        name="conv_module",
    )(z3, z3, w, cb, lg, lb)


def _merge_body(x_ref, g_ref, wg_ref, a_ref, r_ref, c_ref, wb_ref, wo_ref, o_ref):
    x = x_ref[...]
    h = _rmsnorm_bf16(x, g_ref[...])
    merged = None
    for i, br in enumerate((a_ref, r_ref, c_ref)):
        gate = jnp.dot(h, wg_ref[:, i * D_MODEL:(i + 1) * D_MODEL], preferred_element_type=F32)
        proj = jnp.dot(br[...], wb_ref[i], preferred_element_type=F32)
        term = jax.nn.sigmoid(gate) * proj
        merged = term if merged is None else merged + term
    o_ref[...] = x + jnp.dot(merged.astype(BF16), wo_ref[...], preferred_element_type=F32)


def _merge(x2, g, wg, o_nsa, o_ret, o_conv, wb, wo):
    T = x2.shape[0]
    tm = 512
    br = pl.BlockSpec((tm, BRANCH_W), lambda i: (i, 0))
    return pl.pallas_call(
        _merge_body,
        grid=(T // tm,),
        in_specs=[pl.BlockSpec((tm, D_MODEL), lambda i: (i, 0)),
                  pl.BlockSpec((1, D_MODEL), lambda i: (0, 0)),
                  pl.BlockSpec(wg.shape, lambda i: (0, 0)),
                  br, br, br,
                  pl.BlockSpec(wb.shape, lambda i: (0, 0, 0)),
                  pl.BlockSpec(wo.shape, lambda i: (0, 0))],
        out_specs=pl.BlockSpec((tm, D_MODEL), lambda i: (i, 0)),
        out_shape=jax.ShapeDtypeStruct((T, D_MODEL), F32),
        compiler_params=_cparams(("parallel",)),
        name="merge_out_proj",
    )(x2, g, wg, o_nsa, o_ret, o_conv, wb, wo)


FF_CHUNK = 1024


def _ffn_body(x_ref, g_ref, w1_ref, w2_ref, gf_ref, o_ref, *, final_norm):
    x = x_ref[...]
    ms = jnp.mean(x * x, axis=-1, keepdims=True)
    h = (x * lax.rsqrt(ms + EPS) * g_ref[...]).astype(BF16)
    acc = x
    for j in range(D_FF // FF_CHUNK):
        cols = slice(j * FF_CHUNK, (j + 1) * FF_CHUNK)
        a = jnp.maximum(jnp.dot(h, w1_ref[:, cols], preferred_element_type=F32), 0.0)
        acc = acc + jnp.dot((a * a).astype(BF16), w2_ref[cols, :], preferred_element_type=F32)
    if final_norm:
        ms = jnp.mean(acc * acc, axis=-1, keepdims=True)
        acc = acc * lax.rsqrt(ms + EPS) * gf_ref[...]
    o_ref[...] = acc


def _ffn(x2, g, w1, w2, gf, final_norm):
    T = x2.shape[0]
    tm = 512
    vec = pl.BlockSpec((1, D_MODEL), lambda i: (0, 0))
    return pl.pallas_call(
        functools.partial(_ffn_body, final_norm=final_norm),
        grid=(T // tm,),
        in_specs=[pl.BlockSpec((tm, D_MODEL), lambda i: (i, 0)), vec,
                  pl.BlockSpec(w1.shape, lambda i: (0, 0)),
                  pl.BlockSpec(w2.shape, lambda i: (0, 0)), vec],
        out_specs=pl.BlockSpec((tm, D_MODEL), lambda i: (i, 0)),
        out_shape=jax.ShapeDtypeStruct((T, D_MODEL), F32),
        compiler_params=_cparams(("parallel",)),
        name="ffn",
    )(x2, g, w1, w2, gf)


def _rel_bucket(dist):
    n = jnp.maximum(dist, 0)
    max_exact = REL_BUCKETS // 2
    nf = jnp.maximum(n, 1).astype(F32)
    large = max_exact + (jnp.log(nf / max_exact) / math.log(REL_MAX_DIST / max_exact)
                         * (REL_BUCKETS - max_exact)).astype(jnp.int32)
    large = jnp.minimum(large, REL_BUCKETS - 1)
    return jnp.where(n < max_exact, n, large)


def _nsa_tables(rel_table, S):
    n_s = S // SEL_BLOCK
    n_cp = S // CMP_STRIDE
    QB, G, HPG = Q_BLOCK, NSA_GROUPS, NSA_HPG
    tab = rel_table.T.astype(F32) * LOG2E

    def bias(dist, ok=None):
        bucket = _rel_bucket(dist)[None]
        full = jnp.zeros((NSA_HEADS,) + dist.shape, F32)
        for b in range(REL_BUCKETS):
            full = jnp.where(bucket == b, tab[:, b][:, None, None], full)
        if ok is not None:
            full = jnp.where(ok[None], full, NEG)
        J = dist.shape[0]
        return jnp.transpose(full.reshape(G, HPG, J, QB), (0, 2, 1, 3)).reshape(G, J, HPG * QB)

    i = jnp.arange(QB)[None, :]
    et = (jnp.arange(S)[:, None] // SEL_BLOCK == jnp.arange(n_s)[None, :]).astype(BF16)
    cmp_start = jnp.arange(n_cp) * CMP_STRIDE
    sel_start = jnp.arange(n_s) * SEL_BLOCK
    ovt = ((cmp_start[None, :] <= sel_start[:, None] + SEL_BLOCK - 1)
           & (cmp_start[None, :] + CMP_BLOCK - 1 >= sel_start[:, None])
           & (jnp.arange(n_cp)[None, :] < n_cp - 1)).astype(BF16)
    d_near = i + (NEAR_TILES - 1) * QB - jnp.arange(NEAR_TILES * QB)[:, None]
    bnear = bias(d_near, d_near >= 0)
    cfar = bias(jnp.full((1, QB), S + REL_MAX_DIST, jnp.int32))
    d_win = i + WINDOW - jnp.arange(WINDOW + QB)[:, None]
    bwin = bias(d_win, (d_win >= 0) & (d_win < WINDOW))
    d_cmp = i - CMP_STRIDE * (jnp.arange(2 * n_cp)[:, None] - n_cp) - (CMP_BLOCK - 1)
    bcmp = bias(d_cmp, d_cmp >= 0)
    return et, ovt, bnear, cfar, bwin, bcmp


def _retention_tables(S):
    H, DK, C = RET_HEADS, RET_DK, RET_CHUNK
    half = DK // 2
    pos = jnp.arange(S, dtype=F32)
    inv = ROPE_BASE ** (-jnp.arange(half, dtype=F32) / half)
    ang = pos[:, None] * inv[None, :]
    cos = jnp.tile(jnp.cos(ang), (1, H))
    sin = jnp.tile(jnp.sin(ang), (1, H))
    log_g = jnp.log1p(-jnp.exp2(-5.0 - jnp.arange(H, dtype=F32)))
    ar = jnp.arange(C)
    diff = ar[:, None] - ar[None, :]
    decay = jnp.where(diff >= 0, jnp.exp(log_g[:, None, None] * jnp.maximum(diff, 0).astype(F32)), 0.0)
    zeta = jnp.exp(log_g[:, None] * (C - 1 - ar).astype(F32))
    xi = jnp.exp(log_g[:, None] * (ar + 1).astype(F32))
    gch = jnp.exp(log_g * C)
    lane_head = (jnp.arange(H * DK) % (H * half)) // half
    hm = (lane_head[None, :] == jnp.arange(H)[:, None]).astype(F32)
    return (gch, cos, sin, decay, hm[:, None, :], xi[:, :, None] * hm[:, None, :],
            zeta[:, :, None] * hm[:, None, :])


def _pack_w_in(w):
    o = np.cumsum([0, 512, 128, 128, 128, 128, 128, 128, 24, 256, 256, 512, 512, 512, 512, 3072])
    seg = lambda i: w[:, int(o[i]):int(o[i + 1])]

    def halves_major(a):
        a4 = a.reshape(a.shape[0], RET_HEADS, 2, RET_DK // 2)
        return jnp.transpose(a4, (0, 2, 1, 3)).reshape(a.shape[0], RET_HEADS * RET_DK)

    parts = [seg(0) * (NSA_DH ** -0.5 * LOG2E), seg(10), seg(11), seg(12), seg(13),
             halves_major(seg(8)), halves_major(seg(9)) * (RET_DK ** -0.5),
             seg(3), seg(4), seg(5), seg(6),
             seg(7), jnp.zeros((w.shape[0], LANE - 3 * NSA_HEADS), w.dtype), seg(1), seg(2)]
    return jnp.concatenate(parts, axis=1).astype(BF16), seg(14).astype(BF16)


def _pack_compress(pe, w1, w2):
    G, DH, r = NSA_GROUPS, NSA_DH, CMP_BLOCK // CMP_STRIDE
    eye = jnp.eye(G, dtype=w1.dtype)
    pe2 = jnp.tile(pe.reshape(r, CMP_STRIDE, 1, DH), (1, 1, G, 1)).reshape(r, CMP_STRIDE * G * DH)
    w1r = w1.reshape(r, CMP_STRIDE, DH, CMP_HIDDEN)
    w1b = jnp.einsum('rldf,gk->rlgdkf', w1r, eye).reshape(r, CMP_STRIDE * G * DH, G * CMP_HIDDEN)
    w2b = jnp.einsum('fd,gk->gfkd', w2, eye).reshape(G * CMP_HIDDEN, G * DH)
    return pe2, w1b.astype(BF16), w2b.astype(BF16)


def kernel(x, rel_table, norm_mix, w_in, cmp_pe_k, cmp_w1_k, cmp_w2_k, cmp_pe_v, cmp_w1_v, cmp_w2_v, ret_gn, conv_w, conv_b, conv_ln_g, conv_ln_b, w_branch, w_out, norm_mlp, w_ff1, w_ff2, norm_final):
    B, S, D = x.shape
    assert B % RET_SEQS == 0 and S % (2 * SEL_TILE) == 0
    depth = w_in.shape[0]
    T = B * S
    nsa_tabs = _nsa_tables(rel_table, S)
    ret_tabs = _retention_tables(S)
    x2 = x.reshape(T, D)
    for l in range(depth):
        w_main, w_gate = _pack_w_in(w_in[l])
        z2, kc, vc = _in_proj(x2, norm_mix[l][None, :], w_main)
        z3 = z2.reshape(B, S, Z_COLS)
        n_ch = S // CMP_STRIDE
        kc2 = kc.reshape(B, n_ch, CMP_STRIDE * NSA_KV)
        vc2 = vc.reshape(B, n_ch, CMP_STRIDE * NSA_KV)
        kcmp, vcmp = _compress(kc2, vc2, *_pack_compress(cmp_pe_k[l], cmp_w1_k[l], cmp_w2_k[l]),
                               *_pack_compress(cmp_pe_v[l], cmp_w1_v[l], cmp_w2_v[l]))
        kcmp = jnp.transpose(kcmp.reshape(B, n_ch, NSA_GROUPS, NSA_DH), (0, 2, 1, 3))
        vcmpt = jnp.transpose(vcmp.reshape(B, n_ch, NSA_GROUPS, NSA_DH), (0, 2, 3, 1))

        o_nsa = _nsa_attention(z3, kcmp, vcmpt, nsa_tabs)
        o_ret = _retention(z3, ret_tabs, ret_gn[l][None, :])
        conv_w_pad = jnp.concatenate([conv_w[l], jnp.zeros((1, BRANCH_W), conv_w.dtype)], axis=0)
        o_conv = _conv_module(z3, conv_w_pad, conv_b[l][None, :], conv_ln_g[l][None, :], conv_ln_b[l][None, :])
        x2 = _merge(x2, norm_mix[l][None, :], w_gate, o_nsa.reshape(T, BRANCH_W), o_ret.reshape(T, BRANCH_W), o_conv.reshape(T, BRANCH_W),
                    w_branch[l].astype(BF16), w_out[l].astype(BF16))
        x2 = _ffn(x2, norm_mlp[l][None, :], w_ff1[l].astype(BF16), w_ff2[l].astype(BF16), norm_final[None, :],
                  final_norm=(l == depth - 1))
    return x2.reshape(B, S, D)
```

```python
import functools
import math

import jax
import jax.numpy as jnp
import numpy as np
from jax import lax
from jax.experimental import pallas as pl
from jax.experimental.pallas import tpu as pltpu

F32 = jnp.float32
BF16 = jnp.bfloat16

D_MODEL = 1024
BRANCH_W = 512
N_BRANCH = 3
NSA_HEADS = 8
NSA_GROUPS = 2
NSA_HPG = 4
NSA_DH = 64
NSA_KV = NSA_GROUPS * NSA_DH
CMP_BLOCK = 32
CMP_STRIDE = 16
CMP_HIDDEN = 128
SEL_BLOCK = 64
SEL_TOPK = 8
WINDOW = 512
Q_BLOCK = 128
RET_HEADS = 4
RET_DV = 128
RET_DK = 64
RET_CHUNK = 128
ROPE_BASE = 10000.0
CONV_WIDTH = 31
REL_BUCKETS = 32
REL_MAX_DIST = 128
D_FF = 4096
EPS = 1e-6
NEG = -1e30
LOG2E = 1.4426950408889634

LANE = 128
VMEM_LIMIT = 56 * 1024 * 1024

GLU_COLS = 2 * BRANCH_W
COL_QN = 0
COL_VR = 512
COL_GR = 1024
COL_QR = 1536
COL_KR = 1792
COL_KS = 2048
COL_GN = 2560
Z_COLS = 2688
W_COLS = GLU_COLS + Z_COLS + 2 * NSA_KV


def _cparams(sem):
    return pltpu.CompilerParams(dimension_semantics=sem, vmem_limit_bytes=VMEM_LIMIT)


def _rmsnorm_bf16(x, g):
    ms = jnp.mean(x * x, axis=-1, keepdims=True)
    return (x * lax.rsqrt(ms + EPS) * g).astype(BF16)


CONV_HALO = 32
SUBLANES = 8
CONV_ROWS = 64


def _conv_stage(u, u_scr):
    tile = u.shape[0]
    rows = CONV_HALO + tile
    u_scr[0, 0:CONV_HALO, :] = u_scr[0, tile:rows, :]
    u_scr[0, CONV_HALO:rows, :] = u
    for r in range(1, SUBLANES):
        u_scr[r, 0:rows - SUBLANES, :] = u_scr[0, r:rows - SUBLANES + r, :]


def _conv_rows(r0, w_ref, cb_ref, lg_ref, lb_ref, o_ref, u_scr):
    lead = CONV_HALO - (CONV_WIDTH - 1)
    acc = jnp.zeros((CONV_ROWS, u_scr.shape[2]), F32) + cb_ref[...]
    for k in range(CONV_WIDTH):
        start = r0 + (lead + k) // SUBLANES * SUBLANES
        acc = acc + w_ref[k:k + 1, :] * u_scr[(lead + k) % SUBLANES, pl.ds(start, CONV_ROWS), :]
    mu = jnp.mean(acc, axis=-1, keepdims=True)
    d = acc - mu
    var = jnp.mean(d * d, axis=-1, keepdims=True)
    y = d * lax.rsqrt(var + EPS) * lg_ref[...] + lb_ref[...]
    o_ref[pl.ds(r0, CONV_ROWS), :] = (y * jax.nn.sigmoid(y)).astype(o_ref.dtype)


def _in_proj_body(x_ref, g_ref, w_ref, cw_ref, cb_ref, lg_ref, lb_ref, z_ref, kc_ref, vc_ref, oc_ref, u_scr,
                  *, tiles_per_seq):
    tm = x_ref.shape[0]

    @pl.when(pl.program_id(0) % tiles_per_seq == 0)
    def _():
        u_scr[0, tm:tm + CONV_HALO, :] = jnp.zeros((CONV_HALO, u_scr.shape[2]), F32)

    h = _rmsnorm_bf16(x_ref[...], g_ref[...])
    glu = jnp.dot(h, w_ref[:, 0:GLU_COLS], preferred_element_type=F32)
    acc = jnp.dot(h, w_ref[:, GLU_COLS:W_COLS], preferred_element_type=F32)
    z_ref[...] = acc[:, 0:Z_COLS]
    kc_ref[...] = acc[:, Z_COLS:Z_COLS + NSA_KV]
    vc_ref[...] = acc[:, Z_COLS + NSA_KV:Z_COLS + 2 * NSA_KV]
    _conv_stage(glu[:, 0:BRANCH_W] * jax.nn.sigmoid(glu[:, BRANCH_W:GLU_COLS]), u_scr)
    for r0 in range(0, tm, CONV_ROWS):
        _conv_rows(r0, cw_ref, cb_ref, lg_ref, lb_ref, oc_ref, u_scr)


def _in_proj(x2, g, w, seq_len, conv_w, conv_b, conv_lg, conv_lb):
    T = x2.shape[0]
    tm = 512
    assert seq_len % tm == 0

    def const(a):
        return pl.BlockSpec(a.shape, lambda i: (0, 0))

    def rows(width):
        return pl.BlockSpec((tm, width), lambda i: (i, 0))

    return pl.pallas_call(
        functools.partial(_in_proj_body, tiles_per_seq=seq_len // tm),
        grid=(T // tm,),
        in_specs=[rows(D_MODEL), const(g), const(w), const(conv_w), const(conv_b), const(conv_lg), const(conv_lb)],
        out_specs=[rows(Z_COLS), rows(NSA_KV), rows(NSA_KV), rows(BRANCH_W)],
        out_shape=[jax.ShapeDtypeStruct((T, Z_COLS), F32), jax.ShapeDtypeStruct((T, NSA_KV), F32),
                   jax.ShapeDtypeStruct((T, NSA_KV), F32), jax.ShapeDtypeStruct((T, BRANCH_W), BF16)],
        scratch_shapes=[pltpu.VMEM((SUBLANES, CONV_HALO + tm, BRANCH_W), F32)],
        compiler_params=_cparams(("arbitrary",)),
        name="in_proj_conv",
    )(x2, g, w, conv_w, conv_b, conv_lg, conv_lb)


def _compress_body(kc_ref, vc_ref, pek_ref, w1k_ref, w2k_ref, pev_ref, w1v_ref, w2v_ref, ko_ref, vo_ref):
    def one(c_ref, pe_ref, w1_ref, w2_ref, o_ref):
        ch = c_ref[...]
        a = jnp.dot((ch + pe_ref[0:1, :]).astype(BF16), w1_ref[0], preferred_element_type=F32)
        b = jnp.dot((ch + pe_ref[1:2, :]).astype(BF16), w1_ref[1], preferred_element_type=F32)
        n = b.shape[0]
        hid = jax.nn.gelu(a + pltpu.roll(b, n - 1, 0))
        o_ref[...] = jnp.dot(hid.astype(BF16), w2_ref[...], preferred_element_type=F32).astype(o_ref.dtype)

    one(kc_ref, pek_ref, w1k_ref, w2k_ref, ko_ref)
    one(vc_ref, pev_ref, w1v_ref, w2v_ref, vo_ref)


def _compress(kc2, vc2, pek, w1k, w2k, pev, w1v, w2v):
    B, n_ch, wide = kc2.shape
    hid2 = NSA_GROUPS * CMP_HIDDEN
    act = pl.BlockSpec((None, n_ch, wide), lambda b: (b, 0, 0))
    pe = pl.BlockSpec((2, wide), lambda b: (0, 0))
    w1 = pl.BlockSpec((2, wide, hid2), lambda b: (0, 0, 0))
    w2 = pl.BlockSpec((hid2, NSA_KV), lambda b: (0, 0))
    out = pl.BlockSpec((None, n_ch, NSA_KV), lambda b: (b, 0, 0))
    return pl.pallas_call(
        _compress_body,
        grid=(B,),
        in_specs=[act, act, pe, w1, w2, pe, w1, w2],
        out_specs=[out, out],
        out_shape=[jax.ShapeDtypeStruct((B, n_ch, NSA_KV), BF16)] * 2,
        compiler_params=_cparams(("parallel",)),
        name="nsa_compress",
    )(kc2, vc2, pek, w1k, w2k, pev, w1v, w2v)


SEL_TILE = 2 * Q_BLOCK
V_ROWS = NSA_DH + 16
NEAR_TILES = 3


def _nsa_body(q_ref, gn_ref, ks_ref, vs_ref, kw_ref, vw_ref, kcmp_ref, vcmpt_ref,
              et_ref, ovt_ref, bnear_ref, cfar_ref, bwin_ref, bcmp_ref, o_ref,
              kaug_scr, kw_scr, vst_ref, vwt_ref, part_scr, near_scr, sa_scr, sb_scr, pa_scr, pb_scr):
    c = pl.program_id(1)
    n_s = et_ref.shape[1]
    n_cp = kcmp_ref.shape[1]
    QB, DH, HPG = Q_BLOCK, NSA_DH, NSA_HPG
    s_len = ks_ref.shape[0]

    @pl.when(c == 0)
    def _():
        for g in range(NSA_GROUPS):
            kaug_scr[g, 0:s_len, 0:DH] = ks_ref[:, g * DH:(g + 1) * DH].astype(BF16)
            kaug_scr[g, 0:s_len, DH:DH + n_s] = et_ref[...]
            kaug_scr[g, s_len:s_len + SEL_TILE, 0:DH] = jnp.zeros((SEL_TILE, DH), BF16)
            kaug_scr[g, s_len:s_len + SEL_TILE, DH:DH + n_s] = jnp.ones((SEL_TILE, n_s), BF16)
            kw_scr[g, 0:WINDOW, :] = (lax.broadcasted_iota(jnp.int32, (WINDOW, LANE), 1) == DH).astype(BF16)
            kw_scr[g, WINDOW:WINDOW + s_len, 0:DH] = kw_ref[:, g * DH:(g + 1) * DH].astype(BF16)
            kw_scr[g, WINDOW:WINDOW + s_len, DH:LANE] = jnp.zeros((s_len, LANE - DH), BF16)
            vwt_ref[g, :, 0:WINDOW] = jnp.zeros((V_ROWS, WINDOW), BF16)

        def transpose_values(t, _):
            r0 = pl.multiple_of(t * QB, QB)
            for src, dst, pad in ((vs_ref, vst_ref, 0), (vw_ref, vwt_ref, WINDOW)):
                v_t = src[pl.ds(r0, QB), :].T.astype(BF16)
                for g in range(NSA_GROUPS):
                    dst[g, 0:DH, pl.ds(pad + r0, QB)] = v_t[g * DH:(g + 1) * DH]
            return 0

        lax.fori_loop(0, s_len // QB, transpose_values, 0)
        ones_row = (lax.broadcasted_iota(jnp.int32, (V_ROWS - DH, s_len), 0) == 0).astype(BF16)
        for g in range(NSA_GROUPS):
            vst_ref[g, DH:V_ROWS, :] = ones_row
            vwt_ref[g, DH:V_ROWS, WINDOW:WINDOW + s_len] = ones_row

    gates = jax.nn.sigmoid(gn_ref[...].T)
    q_t = [q_ref[:, i * LANE:(i + 1) * LANE].T for i in range(BRANCH_W // LANE)]

    blk = lax.broadcasted_iota(jnp.int32, (n_s, QB), 0)
    cur = (c * QB + lax.broadcasted_iota(jnp.int32, (n_s, QB), 1)) // SEL_BLOCK
    valid = blk <= cur
    forced = (blk == 0) | (blk == cur) | (blk == cur - 1)
    blk_f = blk.astype(F32)

    cmp_off = pl.multiple_of(n_cp - c * (QB // CMP_STRIDE), QB // CMP_STRIDE)
    n_far = jnp.maximum(c - 1, 0)
    n_w = WINDOW // QB + 1

    def normalised(acc):
        return acc[0:DH] * (1.0 / jnp.maximum(acc[DH:DH + 1], 1e-30))

    def gate_row(g, branch):
        return jnp.concatenate([gates[3 * (g * HPG + h) + branch:3 * (g * HPG + h) + branch + 1, :]
                                for h in range(HPG)], axis=1)

    near_tiles = ((n_far - 1, n_far % 2 == 1), (c - 1, c >= 1), (c, None))
    near_rows = [pl.multiple_of(t * QB if ok is None else jnp.where(ok, t * QB, s_len), QB) for t, ok in near_tiles]
    near_cols = [pl.multiple_of(jnp.maximum(t, 0) * QB, QB) for t, _ in near_tiles]
    win_flag = jnp.where(lax.broadcasted_iota(jnp.int32, (LANE - DH, HPG * QB), 0) == 0, NEG, 0.0).astype(BF16)
    w0 = pl.multiple_of(c * QB, QB)
    groups = range(NSA_GROUPS)

    qts = [jnp.concatenate([q_t[(g * HPG + h) * DH // LANE][((g * HPG + h) * DH) % LANE:((g * HPG + h) * DH) % LANE + DH]
                            for h in range(HPG)], axis=1).astype(BF16) for g in groups]
    s_cs = [jnp.dot(kcmp_ref[g], qts[g], preferred_element_type=F32) for g in groups]
    s_ws = [jnp.dot(kw_scr[g, pl.ds(w0, WINDOW + QB), :], jnp.concatenate([qts[g], win_flag], axis=0),
                    preferred_element_type=F32) for g in groups]

    qas, o_cs = [], []
    for g in groups:
        s_c = s_cs[g] + bcmp_ref[g, pl.ds(cmp_off, n_cp), :]
        e_c = jnp.exp2(s_c - jnp.maximum(jnp.max(s_c, axis=0, keepdims=True), 0.1 * NEG))
        p_c = e_c * (1.0 / jnp.maximum(jnp.sum(e_c, axis=0, keepdims=True), 1e-30))
        o_cs.append(jnp.dot(vcmpt_ref[g], p_c.astype(BF16), preferred_element_type=F32))

        p_sum = p_c[:, 0:QB] + p_c[:, QB:2 * QB] + p_c[:, 2 * QB:3 * QB] + p_c[:, 3 * QB:4 * QB]
        p_hi = p_sum.astype(BF16)
        p_lo = (p_sum - p_hi.astype(F32)).astype(BF16)
        imp = (jnp.dot(ovt_ref[...], p_hi, preferred_element_type=F32)
               + jnp.dot(ovt_ref[...], p_lo, preferred_element_type=F32))
        score = jnp.where(forced, imp + 1e4, jnp.where(valid, imp, -1e4))
        selneg = jnp.full((n_s, QB), NEG, F32)
        for _ in range(min(SEL_TOPK, n_s)):
            m = jnp.max(score, axis=0, keepdims=True)
            first = jnp.min(jnp.where(score == m, blk_f, float(n_s)), axis=0, keepdims=True)
            pick = blk_f == first
            selneg = jnp.where(pick, 0.0, selneg)
            score = jnp.where(pick, -jnp.inf, score)
        qas.append(jnp.concatenate([qts[g], jnp.concatenate([selneg.astype(BF16)] * HPG, axis=1)], axis=0))

    s_ns = [jnp.dot(jnp.concatenate([kaug_scr[g, pl.ds(r0, QB), :] for r0 in near_rows], axis=0), qas[g],
                    preferred_element_type=F32) for g in groups]

    for g in groups:
        s_w = s_ws[g] + bwin_ref[g]
        e_w = jnp.exp2(s_w - jnp.max(s_w, axis=0, keepdims=True))
        o_w = normalised(jnp.dot(vwt_ref[g, :, pl.ds(w0, WINDOW + QB)], e_w.astype(BF16),
                                 preferred_element_type=F32))
        part_scr[g] = gate_row(g, 0) * o_cs[g] + gate_row(g, 2) * o_w

    near_m = []
    for g in groups:
        s_n = s_ns[g] + bnear_ref[g]
        m_n = jnp.max(s_n, axis=0, keepdims=True)
        near_m.append(m_n)
        v_n = jnp.concatenate([vst_ref[g, :, pl.ds(r0, QB)] for r0 in near_cols], axis=1)
        near_scr[g] = jnp.dot(v_n, jnp.exp2(s_n - m_n).astype(BF16), preferred_element_type=F32)

    n_it = n_far // 2

    def scores(kt, g, dst):
        r0 = pl.multiple_of(jnp.where(kt < n_it, kt * SEL_TILE, s_len), SEL_TILE)
        dst[g] = jnp.dot(kaug_scr[g, pl.ds(r0, SEL_TILE), :], qas[g], preferred_element_type=F32)

    def values(kt, g, p_src):
        r0 = pl.multiple_of(jnp.clip(kt, 0, jnp.maximum(n_it - 1, 0)) * SEL_TILE, SEL_TILE)
        return jnp.dot(vst_ref[g, :, pl.ds(r0, SEL_TILE)], p_src[g], preferred_element_type=F32)

    def half(kt, carries, src, dst, p_src, p_dst):
        for g in groups:
            scores(kt + 1, g, dst)
        pv = [values(kt - 1, g, p_src) for g in groups]
        out = []
        for g in groups:
            m_i, acc, alpha_prev = carries[g]
            s = src[g]
            m_n = jnp.maximum(m_i, jnp.max(s, axis=0, keepdims=True))
            p_dst[g] = jnp.exp2(s - m_n).astype(BF16)
            out.append((m_n, alpha_prev * acc + pv[g], jnp.exp2(m_i - m_n)))
        return tuple(out)

    def far(j, carries):
        first = half(2 * j, carries, sa_scr, sb_scr, pb_scr, pa_scr)
        return half(2 * j + 1, first, sb_scr, sa_scr, pa_scr, pb_scr)

    for g in groups:
        pb_scr[g] = jnp.zeros((SEL_TILE, HPG * QB), BF16)
        scores(0, g, sa_scr)
    init = (jnp.full((1, HPG * QB), NEG, F32), jnp.zeros((V_ROWS, HPG * QB), F32), jnp.ones((1, HPG * QB), F32))
    n_trips = (n_it + 1) // 2
    piped = lax.fori_loop(0, n_trips, far, (init,) * NSA_GROUPS)
    carries = [(m_i, alpha_prev * acc + values(2 * n_trips - 1, g, pb_scr))
               for g, (m_i, acc, alpha_prev) in enumerate(piped)]

    for g in range(NSA_GROUPS):
        m_f, acc_f = carries[g]
        m_f = m_f + cfar_ref[g]
        m_s = jnp.maximum(m_f, near_m[g])
        acc_s = jnp.exp2(m_f - m_s) * acc_f + jnp.exp2(near_m[g] - m_s) * near_scr[g]
        o_g = part_scr[g] + gate_row(g, 1) * normalised(acc_s)
        for i in range(HPG * DH // LANE):
            pair = jnp.concatenate([o_g[:, (2 * i) * QB:(2 * i + 1) * QB], o_g[:, (2 * i + 1) * QB:(2 * i + 2) * QB]], axis=0)
            col = g * HPG * DH + i * LANE
            o_ref[:, col:col + LANE] = pair.T.astype(o_ref.dtype)


def _nsa_attention(z3, kcmp, vcmpt, tabs):
    B, S, _ = z3.shape
    et, ovt, bnear, cfar, bwin, bcmp = tabs

    def zcol(width, col):
        return pl.BlockSpec((None, Q_BLOCK, width), lambda b, c: (b, c, col // width))

    def zfull(col):
        return pl.BlockSpec((None, S, NSA_KV), lambda b, c: (b, 0, col // NSA_KV))

    def per_batch(a):
        nd = a.ndim - 1
        return pl.BlockSpec((None,) + a.shape[1:], lambda b, c: (b,) + (0,) * nd)

    def const(a):
        nd = a.ndim
        return pl.BlockSpec(a.shape, lambda b, c: (0,) * nd)

    tile = (NSA_GROUPS, SEL_TILE, NSA_HPG * Q_BLOCK)
    return pl.pallas_call(
        _nsa_body,
        grid=(B, S // Q_BLOCK),
        in_specs=[zcol(BRANCH_W, COL_QN), zcol(LANE, COL_GN),
                  zfull(COL_KS), zfull(COL_KS + NSA_KV), zfull(COL_KS + 2 * NSA_KV), zfull(COL_KS + 3 * NSA_KV),
                  per_batch(kcmp), per_batch(vcmpt),
                  const(et), const(ovt), const(bnear), const(cfar), const(bwin), const(bcmp)],
        out_specs=pl.BlockSpec((None, Q_BLOCK, BRANCH_W), lambda b, c: (b, c, 0)),
        out_shape=jax.ShapeDtypeStruct((B, S, BRANCH_W), BF16),
        scratch_shapes=[pltpu.VMEM((NSA_GROUPS, S + SEL_TILE, NSA_DH + et.shape[1]), BF16),
                        pltpu.VMEM((NSA_GROUPS, WINDOW + S, LANE), BF16),
                        pltpu.VMEM((NSA_GROUPS, V_ROWS, S), BF16), pltpu.VMEM((NSA_GROUPS, V_ROWS, WINDOW + S), BF16),
                        pltpu.VMEM((NSA_GROUPS, NSA_DH, NSA_HPG * Q_BLOCK), F32),
                        pltpu.VMEM((NSA_GROUPS, V_ROWS, NSA_HPG * Q_BLOCK), F32),
                        pltpu.VMEM(tile, F32), pltpu.VMEM(tile, F32), pltpu.VMEM(tile, BF16), pltpu.VMEM(tile, BF16)],
        compiler_params=_cparams(("parallel", "arbitrary")),
        name="nsa_attention",
    )(z3, z3, z3, z3, z3, z3, kcmp, vcmpt, et, ovt, bnear, cfar, bwin, bcmp)


RET_SEQS = 4


def _retention_body(gch_ref, q_ref, k_ref, v_ref, g_ref, cos_ref, sin_ref, decay_ref, hm_ref, xi_ref,
                    zeta_ref, gn_ref, o_ref, state_scr):
    @pl.when(pl.program_id(1) == 0)
    def _():
        state_scr[...] = jnp.zeros_like(state_scr)

    half = RET_HEADS * RET_DK // 2
    cos, sin = cos_ref[...], sin_ref[...]

    def rot(x):
        x1, x2 = x[:, :half], x[:, half:]
        return jnp.concatenate([x1 * cos - x2 * sin, x1 * sin + x2 * cos], axis=1)

    nt = (((1,), (1,)), ((), ()))
    tn = (((0,), (0,)), ((), ()))
    pairs = [(r, h) for r in range(RET_SEQS) for h in range(RET_HEADS)]
    qr = [rot(q_ref[r]) for r in range(RET_SEQS)]
    kr = [rot(k_ref[r]) for r in range(RET_SEQS)]
    kb = [x.astype(BF16) for x in kr]
    vh = {(r, h): v_ref[r, :, h * RET_DV:(h + 1) * RET_DV].astype(BF16) for r, h in pairs}
    inner = {(r, h): lax.dot_general((qr[r] * hm_ref[h]).astype(BF16), kb[r], nt, preferred_element_type=F32)
             for r, h in pairs}
    state = {(r, h): state_scr[r, h] for r, h in pairs}
    cross = {(r, h): jnp.dot((qr[r] * xi_ref[h]).astype(BF16), state[r, h].astype(BF16),
                             preferred_element_type=F32) for r, h in pairs}
    kv = {(r, h): lax.dot_general((kr[r] * zeta_ref[h]).astype(BF16), vh[r, h], tn, preferred_element_type=F32)
          for r, h in pairs}
    for r, h in pairs:
        state_scr[r, h] = gch_ref[h] * state[r, h] + kv[r, h]
    for r, h in pairs:
        cols = slice(h * RET_DV, (h + 1) * RET_DV)
        o = jnp.dot((inner[r, h] * decay_ref[h]).astype(BF16), vh[r, h], preferred_element_type=F32) + cross[r, h]
        mu = jnp.mean(o, axis=-1, keepdims=True)
        d = o - mu
        var = jnp.mean(d * d, axis=-1, keepdims=True)
        y = d * lax.rsqrt(var + EPS) * gn_ref[:, cols]
        gate = g_ref[r, :, cols]
        o_ref[r, :, cols] = (gate * jax.nn.sigmoid(gate) * y).astype(o_ref.dtype)


def _retention(z3, tabs, gn):
    B, S, _ = z3.shape
    C = RET_CHUNK
    gch, cos, sin, decay, hm, xi, zeta = tabs
    qk_w = RET_HEADS * RET_DK

    def zcol(width, col):
        return pl.BlockSpec((RET_SEQS, C, width), lambda b, n: (b, n, col // width))

    def const(a):
        nd = a.ndim
        return pl.BlockSpec(a.shape, lambda b, n: (0,) * nd)

    pos = pl.BlockSpec((C, qk_w // 2), lambda b, n: (n, 0))
    return pl.pallas_call(
        _retention_body,
        grid=(B // RET_SEQS, S // C),
        in_specs=[pl.BlockSpec(memory_space=pltpu.SMEM),
                  zcol(qk_w, COL_QR), zcol(qk_w, COL_KR), zcol(BRANCH_W, COL_VR), zcol(BRANCH_W, COL_GR),
                  pos, pos, const(decay), const(hm), const(xi), const(zeta), const(gn)],
        out_specs=pl.BlockSpec((RET_SEQS, C, BRANCH_W), lambda b, n: (b, n, 0)),
        out_shape=jax.ShapeDtypeStruct((B, S, BRANCH_W), BF16),
        scratch_shapes=[pltpu.VMEM((RET_SEQS, RET_HEADS, qk_w, RET_DV), F32)],
        compiler_params=_cparams(("parallel", "arbitrary")),
        name="retention",
    )(gch, z3, z3, z3, z3, cos, sin, decay, hm, xi, zeta, gn)


def _merge_body(x_ref, g_ref, wg_ref, a_ref, r_ref, c_ref, wb_ref, wo_ref, o_ref):
    x = x_ref[...]
    h = _rmsnorm_bf16(x, g_ref[...])
    merged = None
    for i, br in enumerate((a_ref, r_ref, c_ref)):
        gate = jnp.dot(h, wg_ref[:, i * D_MODEL:(i + 1) * D_MODEL], preferred_element_type=F32)
        proj = jnp.dot(br[...], wb_ref[i], preferred_element_type=F32)
        term = jax.nn.sigmoid(gate) * proj
        merged = term if merged is None else merged + term
    o_ref[...] = x + jnp.dot(merged.astype(BF16), wo_ref[...], preferred_element_type=F32)


def _merge(x2, g, wg, o_nsa, o_ret, o_conv, wb, wo):
    T = x2.shape[0]
    tm = 512
    br = pl.BlockSpec((tm, BRANCH_W), lambda i: (i, 0))
    return pl.pallas_call(
        _merge_body,
        grid=(T // tm,),
        in_specs=[pl.BlockSpec((tm, D_MODEL), lambda i: (i, 0)),
                  pl.BlockSpec((1, D_MODEL), lambda i: (0, 0)),
                  pl.BlockSpec(wg.shape, lambda i: (0, 0)),
                  br, br, br,
                  pl.BlockSpec(wb.shape, lambda i: (0, 0, 0)),
                  pl.BlockSpec(wo.shape, lambda i: (0, 0))],
        out_specs=pl.BlockSpec((tm, D_MODEL), lambda i: (i, 0)),
        out_shape=jax.ShapeDtypeStruct((T, D_MODEL), F32),
        compiler_params=_cparams(("parallel",)),
        name="merge_out_proj",
    )(x2, g, wg, o_nsa, o_ret, o_conv, wb, wo)


FF_CHUNK = 1024


def _ffn_body(x_ref, g_ref, w1_ref, w2_ref, gf_ref, o_ref, *, final_norm):
    x = x_ref[...]
    ms = jnp.mean(x * x, axis=-1, keepdims=True)
    h = (x * lax.rsqrt(ms + EPS) * g_ref[...]).astype(BF16)
    acc = x
    for j in range(D_FF // FF_CHUNK):
        cols = slice(j * FF_CHUNK, (j + 1) * FF_CHUNK)
        a = jnp.maximum(jnp.dot(h, w1_ref[:, cols], preferred_element_type=F32), 0.0)
        acc = acc + jnp.dot((a * a).astype(BF16), w2_ref[cols, :], preferred_element_type=F32)
    if final_norm:
        ms = jnp.mean(acc * acc, axis=-1, keepdims=True)
        acc = acc * lax.rsqrt(ms + EPS) * gf_ref[...]
    o_ref[...] = acc


def _ffn(x2, g, w1, w2, gf, final_norm):
    T = x2.shape[0]
    tm = 512
    vec = pl.BlockSpec((1, D_MODEL), lambda i: (0, 0))
    return pl.pallas_call(
        functools.partial(_ffn_body, final_norm=final_norm),
        grid=(T // tm,),
        in_specs=[pl.BlockSpec((tm, D_MODEL), lambda i: (i, 0)), vec,
                  pl.BlockSpec(w1.shape, lambda i: (0, 0)),
                  pl.BlockSpec(w2.shape, lambda i: (0, 0)), vec],
        out_specs=pl.BlockSpec((tm, D_MODEL), lambda i: (i, 0)),
        out_shape=jax.ShapeDtypeStruct((T, D_MODEL), F32),
        compiler_params=_cparams(("parallel",)),
        name="ffn",
    )(x2, g, w1, w2, gf)


def _rel_bucket(dist):
    n = jnp.maximum(dist, 0)
    max_exact = REL_BUCKETS // 2
    nf = jnp.maximum(n, 1).astype(F32)
    large = max_exact + (jnp.log(nf / max_exact) / math.log(REL_MAX_DIST / max_exact)
                         * (REL_BUCKETS - max_exact)).astype(jnp.int32)
    large = jnp.minimum(large, REL_BUCKETS - 1)
    return jnp.where(n < max_exact, n, large)


def _nsa_tables(rel_table, S):
    n_s = S // SEL_BLOCK
    n_cp = S // CMP_STRIDE
    QB, G, HPG = Q_BLOCK, NSA_GROUPS, NSA_HPG
    tab = rel_table.T.astype(F32) * LOG2E

    def bias(dist, ok=None):
        bucket = _rel_bucket(dist)[None]
        full = jnp.zeros((NSA_HEADS,) + dist.shape, F32)
        for b in range(REL_BUCKETS):
            full = jnp.where(bucket == b, tab[:, b][:, None, None], full)
        if ok is not None:
            full = jnp.where(ok[None], full, NEG)
        J = dist.shape[0]
        return jnp.transpose(full.reshape(G, HPG, J, QB), (0, 2, 1, 3)).reshape(G, J, HPG * QB)

    i = jnp.arange(QB)[None, :]
    et = (jnp.arange(S)[:, None] // SEL_BLOCK == jnp.arange(n_s)[None, :]).astype(BF16)
    cmp_start = jnp.arange(n_cp) * CMP_STRIDE
    sel_start = jnp.arange(n_s) * SEL_BLOCK
    ovt = ((cmp_start[None, :] <= sel_start[:, None] + SEL_BLOCK - 1)
           & (cmp_start[None, :] + CMP_BLOCK - 1 >= sel_start[:, None])
           & (jnp.arange(n_cp)[None, :] < n_cp - 1)).astype(BF16)
    d_near = i + (NEAR_TILES - 1) * QB - jnp.arange(NEAR_TILES * QB)[:, None]
    bnear = bias(d_near, d_near >= 0)
    cfar = bias(jnp.full((1, QB), S + REL_MAX_DIST, jnp.int32))
    d_win = i + WINDOW - jnp.arange(WINDOW + QB)[:, None]
    bwin = bias(d_win, (d_win >= 0) & (d_win < WINDOW))
    d_cmp = i - CMP_STRIDE * (jnp.arange(2 * n_cp)[:, None] - n_cp) - (CMP_BLOCK - 1)
    bcmp = bias(d_cmp, d_cmp >= 0)
    return et, ovt, bnear, cfar, bwin, bcmp


def _retention_tables(S):
    H, DK, C = RET_HEADS, RET_DK, RET_CHUNK
    half = DK // 2
    pos = jnp.arange(S, dtype=F32)
    inv = ROPE_BASE ** (-jnp.arange(half, dtype=F32) / half)
    ang = pos[:, None] * inv[None, :]
    cos = jnp.tile(jnp.cos(ang), (1, H))
    sin = jnp.tile(jnp.sin(ang), (1, H))
    log_g = jnp.log1p(-jnp.exp2(-5.0 - jnp.arange(H, dtype=F32)))
    ar = jnp.arange(C)
    diff = ar[:, None] - ar[None, :]
    decay = jnp.where(diff >= 0, jnp.exp(log_g[:, None, None] * jnp.maximum(diff, 0).astype(F32)), 0.0)
    zeta = jnp.exp(log_g[:, None] * (C - 1 - ar).astype(F32))
    xi = jnp.exp(log_g[:, None] * (ar + 1).astype(F32))
    gch = jnp.exp(log_g * C)
    lane_head = (jnp.arange(H * DK) % (H * half)) // half
    hm = (lane_head[None, :] == jnp.arange(H)[:, None]).astype(F32)
    return (gch, cos, sin, decay, hm[:, None, :], xi[:, :, None] * hm[:, None, :],
            zeta[:, :, None] * hm[:, None, :])


def _pack_w_in(w):
    o = np.cumsum([0, 512, 128, 128, 128, 128, 128, 128, 24, 256, 256, 512, 512, 512, 512, 3072])
    seg = lambda i: w[:, int(o[i]):int(o[i + 1])]

    def halves_major(a):
        a4 = a.reshape(a.shape[0], RET_HEADS, 2, RET_DK // 2)
        return jnp.transpose(a4, (0, 2, 1, 3)).reshape(a.shape[0], RET_HEADS * RET_DK)

    parts = [seg(12), seg(13), seg(0) * (NSA_DH ** -0.5 * LOG2E), seg(10), seg(11),
             halves_major(seg(8)), halves_major(seg(9)) * (RET_DK ** -0.5),
             seg(3), seg(4), seg(5), seg(6),
             seg(7), jnp.zeros((w.shape[0], LANE - 3 * NSA_HEADS), w.dtype), seg(1), seg(2)]
    return jnp.concatenate(parts, axis=1).astype(BF16), seg(14).astype(BF16)


def _pack_compress(pe, w1, w2):
    G, DH, r = NSA_GROUPS, NSA_DH, CMP_BLOCK // CMP_STRIDE
    eye = jnp.eye(G, dtype=w1.dtype)
    pe2 = jnp.tile(pe.reshape(r, CMP_STRIDE, 1, DH), (1, 1, G, 1)).reshape(r, CMP_STRIDE * G * DH)
    w1r = w1.reshape(r, CMP_STRIDE, DH, CMP_HIDDEN)
    w1b = jnp.einsum('rldf,gk->rlgdkf', w1r, eye).reshape(r, CMP_STRIDE * G * DH, G * CMP_HIDDEN)
    w2b = jnp.einsum('fd,gk->gfkd', w2, eye).reshape(G * CMP_HIDDEN, G * DH)
    return pe2, w1b.astype(BF16), w2b.astype(BF16)


def kernel(x, rel_table, norm_mix, w_in, cmp_pe_k, cmp_w1_k, cmp_w2_k, cmp_pe_v, cmp_w1_v, cmp_w2_v, ret_gn, conv_w, conv_b, conv_ln_g, conv_ln_b, w_branch, w_out, norm_mlp, w_ff1, w_ff2, norm_final):
    B, S, D = x.shape
    assert B % RET_SEQS == 0 and S % (2 * SEL_TILE) == 0
    depth = w_in.shape[0]
    T = B * S
    nsa_tabs = _nsa_tables(rel_table, S)
    ret_tabs = _retention_tables(S)
    x2 = x.reshape(T, D)
    for l in range(depth):
        w_main, w_gate = _pack_w_in(w_in[l])
        conv_w_pad = jnp.concatenate([conv_w[l], jnp.zeros((1, BRANCH_W), conv_w.dtype)], axis=0)
        z2, kc, vc, o_conv = _in_proj(x2, norm_mix[l][None, :], w_main, S, conv_w_pad, conv_b[l][None, :],
                                      conv_ln_g[l][None, :], conv_ln_b[l][None, :])
        z3 = z2.reshape(B, S, Z_COLS)
        n_ch = S // CMP_STRIDE
        kc2 = kc.reshape(B, n_ch, CMP_STRIDE * NSA_KV)
        vc2 = vc.reshape(B, n_ch, CMP_STRIDE * NSA_KV)
        kcmp, vcmp = _compress(kc2, vc2, *_pack_compress(cmp_pe_k[l], cmp_w1_k[l], cmp_w2_k[l]),
                               *_pack_compress(cmp_pe_v[l], cmp_w1_v[l], cmp_w2_v[l]))
        kcmp = jnp.transpose(kcmp.reshape(B, n_ch, NSA_GROUPS, NSA_DH), (0, 2, 1, 3))
        vcmpt = jnp.transpose(vcmp.reshape(B, n_ch, NSA_GROUPS, NSA_DH), (0, 2, 3, 1))

        o_nsa = _nsa_attention(z3, kcmp, vcmpt, nsa_tabs)
        o_ret = _retention(z3, ret_tabs, ret_gn[l][None, :])
        x2 = _merge(x2, norm_mix[l][None, :], w_gate, o_nsa.reshape(T, BRANCH_W), o_ret.reshape(T, BRANCH_W), o_conv.reshape(T, BRANCH_W),
                    w_branch[l].astype(BF16), w_out[l].astype(BF16))
        x2 = _ffn(x2, norm_mlp[l][None, :], w_ff1[l].astype(BF16), w_ff2[l].astype(BF16), norm_final[None, :],
                  final_norm=(l == depth - 1))
    return x2.reshape(B, S, D)
```

```python
import functools
import math

import jax
import jax.numpy as jnp
import numpy as np
from jax import lax
from jax.experimental import pallas as pl
from jax.experimental.pallas import tpu as pltpu

F32 = jnp.float32
BF16 = jnp.bfloat16

D_MODEL = 1024
BRANCH_W = 512
N_BRANCH = 3
NSA_HEADS = 8
NSA_GROUPS = 2
NSA_HPG = 4
NSA_DH = 64
NSA_KV = NSA_GROUPS * NSA_DH
CMP_BLOCK = 32
CMP_STRIDE = 16
CMP_HIDDEN = 128
SEL_BLOCK = 64
SEL_TOPK = 8
WINDOW = 512
Q_BLOCK = 128
RET_HEADS = 4
RET_DV = 128
RET_DK = 64
RET_CHUNK = 128
ROPE_BASE = 10000.0
CONV_WIDTH = 31
REL_BUCKETS = 32
REL_MAX_DIST = 128
D_FF = 4096
EPS = 1e-6
NEG = -1e30
LOG2E = 1.4426950408889634

LANE = 128
VMEM_LIMIT = 56 * 1024 * 1024

GLU_COLS = 2 * BRANCH_W
QT_COLS = BRANCH_W + LANE
COL_VR = 0
COL_GR = 512
COL_QR = 1024
COL_KR = 1280
COL_KS = 1536
Z_COLS = 2048
W_COLS = GLU_COLS + QT_COLS + Z_COLS + 2 * NSA_KV


def _cparams(sem):
    return pltpu.CompilerParams(dimension_semantics=sem, vmem_limit_bytes=VMEM_LIMIT)


def _rmsnorm_bf16(x, g):
    ms = jnp.mean(x * x, axis=-1, keepdims=True)
    return (x * lax.rsqrt(ms + EPS) * g).astype(BF16)


CONV_HALO = 32
SUBLANES = 8
CONV_ROWS = 64


def _conv_stage(u, u_scr):
    tile = u.shape[0]
    rows = CONV_HALO + tile
    u_scr[0, 0:CONV_HALO, :] = u_scr[0, tile:rows, :]
    u_scr[0, CONV_HALO:rows, :] = u
    for r in range(1, SUBLANES):
        u_scr[r, 0:rows - SUBLANES, :] = u_scr[0, r:rows - SUBLANES + r, :]


def _conv_rows(r0, w_ref, cb_ref, lg_ref, lb_ref, o_ref, u_scr):
    lead = CONV_HALO - (CONV_WIDTH - 1)
    acc = jnp.zeros((CONV_ROWS, u_scr.shape[2]), F32) + cb_ref[...]
    for k in range(CONV_WIDTH):
        start = r0 + (lead + k) // SUBLANES * SUBLANES
        acc = acc + w_ref[k:k + 1, :] * u_scr[(lead + k) % SUBLANES, pl.ds(start, CONV_ROWS), :]
    mu = jnp.mean(acc, axis=-1, keepdims=True)
    d = acc - mu
    var = jnp.mean(d * d, axis=-1, keepdims=True)
    y = d * lax.rsqrt(var + EPS) * lg_ref[...] + lb_ref[...]
    o_ref[pl.ds(r0, CONV_ROWS), :] = (y * jax.nn.sigmoid(y)).astype(o_ref.dtype)


def _in_proj_body(x_ref, g_ref, w_ref, cw_ref, cb_ref, lg_ref, lb_ref, z_ref, kc_ref, vc_ref, oc_ref, qt_ref, gnt_ref,
                  u_scr, *, tiles_per_seq):
    tm = x_ref.shape[0]

    @pl.when(pl.program_id(0) % tiles_per_seq == 0)
    def _():
        u_scr[0, tm:tm + CONV_HALO, :] = jnp.zeros((CONV_HALO, u_scr.shape[2]), F32)

    h = _rmsnorm_bf16(x_ref[...], g_ref[...])
    glu = jnp.dot(h, w_ref[:, 0:GLU_COLS], preferred_element_type=F32)
    acc = jnp.dot(h, w_ref[:, GLU_COLS:W_COLS], preferred_element_type=F32)
    qt_ref[...] = acc[:, 0:BRANCH_W].T.astype(qt_ref.dtype)
    gnt_ref[...] = acc[:, BRANCH_W:QT_COLS].T
    z_ref[...] = acc[:, QT_COLS:QT_COLS + Z_COLS]
    kc_ref[...] = acc[:, QT_COLS + Z_COLS:QT_COLS + Z_COLS + NSA_KV]
    vc_ref[...] = acc[:, QT_COLS + Z_COLS + NSA_KV:QT_COLS + Z_COLS + 2 * NSA_KV]
    _conv_stage(glu[:, 0:BRANCH_W] * jax.nn.sigmoid(glu[:, BRANCH_W:GLU_COLS]), u_scr)
    for r0 in range(0, tm, CONV_ROWS):
        _conv_rows(r0, cw_ref, cb_ref, lg_ref, lb_ref, oc_ref, u_scr)


def _in_proj(x2, g, w, seq_len, conv_w, conv_b, conv_lg, conv_lb):
    T = x2.shape[0]
    tm = 512
    assert seq_len % tm == 0

    def const(a):
        return pl.BlockSpec(a.shape, lambda i: (0, 0))

    def rows(width):
        return pl.BlockSpec((tm, width), lambda i: (i, 0))

    return pl.pallas_call(
        functools.partial(_in_proj_body, tiles_per_seq=seq_len // tm),
        grid=(T // tm,),
        in_specs=[rows(D_MODEL), const(g), const(w), const(conv_w), const(conv_b), const(conv_lg), const(conv_lb)],
        out_specs=[rows(Z_COLS), rows(NSA_KV), rows(NSA_KV), rows(BRANCH_W),
                   pl.BlockSpec((BRANCH_W, tm), lambda i: (0, i)), pl.BlockSpec((LANE, tm), lambda i: (0, i))],
        out_shape=[jax.ShapeDtypeStruct((T, Z_COLS), F32), jax.ShapeDtypeStruct((T, NSA_KV), F32),
                   jax.ShapeDtypeStruct((T, NSA_KV), F32), jax.ShapeDtypeStruct((T, BRANCH_W), BF16),
                   jax.ShapeDtypeStruct((BRANCH_W, T), BF16), jax.ShapeDtypeStruct((LANE, T), F32)],
        scratch_shapes=[pltpu.VMEM((SUBLANES, CONV_HALO + tm, BRANCH_W), F32)],
        compiler_params=_cparams(("arbitrary",)),
        name="in_proj_conv",
    )(x2, g, w, conv_w, conv_b, conv_lg, conv_lb)


def _compress_body(kc_ref, vc_ref, pek_ref, w1k_ref, w2k_ref, pev_ref, w1v_ref, w2v_ref, ko_ref, vo_ref):
    def one(c_ref, pe_ref, w1_ref, w2_ref, o_ref):
        ch = c_ref[...]
        a = jnp.dot((ch + pe_ref[0:1, :]).astype(BF16), w1_ref[0], preferred_element_type=F32)
        b = jnp.dot((ch + pe_ref[1:2, :]).astype(BF16), w1_ref[1], preferred_element_type=F32)
        n = b.shape[0]
        hid = jax.nn.gelu(a + pltpu.roll(b, n - 1, 0))
        o_ref[...] = jnp.dot(hid.astype(BF16), w2_ref[...], preferred_element_type=F32).astype(o_ref.dtype)

    one(kc_ref, pek_ref, w1k_ref, w2k_ref, ko_ref)
    one(vc_ref, pev_ref, w1v_ref, w2v_ref, vo_ref)


def _compress(kc2, vc2, pek, w1k, w2k, pev, w1v, w2v):
    B, n_ch, wide = kc2.shape
    hid2 = NSA_GROUPS * CMP_HIDDEN
    act = pl.BlockSpec((None, n_ch, wide), lambda b: (b, 0, 0))
    pe = pl.BlockSpec((2, wide), lambda b: (0, 0))
    w1 = pl.BlockSpec((2, wide, hid2), lambda b: (0, 0, 0))
    w2 = pl.BlockSpec((hid2, NSA_KV), lambda b: (0, 0))
    out = pl.BlockSpec((None, n_ch, NSA_KV), lambda b: (b, 0, 0))
    return pl.pallas_call(
        _compress_body,
        grid=(B,),
        in_specs=[act, act, pe, w1, w2, pe, w1, w2],
        out_specs=[out, out],
        out_shape=[jax.ShapeDtypeStruct((B, n_ch, NSA_KV), BF16)] * 2,
        compiler_params=_cparams(("parallel",)),
        name="nsa_compress",
    )(kc2, vc2, pek, w1k, w2k, pev, w1v, w2v)


SEL_TILE = 2 * Q_BLOCK
V_ROWS = NSA_DH + 16
NEAR_TILES = 3


def _nsa_body(q_ref, gn_ref, ks_ref, vs_ref, kw_ref, vw_ref, kcmp_ref, vcmpt_ref,
              et_ref, ovt_ref, bnear_ref, cfar_ref, bwin_ref, bcmp_ref, o_ref,
              kaug_scr, kw_scr, vst_ref, vwt_ref, part_scr, near_scr, sa_scr, sb_scr, pa_scr, pb_scr):
    c = pl.program_id(1)
    n_s = et_ref.shape[1]
    n_cp = kcmp_ref.shape[1]
    QB, DH, HPG = Q_BLOCK, NSA_DH, NSA_HPG
    s_len = ks_ref.shape[0]

    @pl.when(c == 0)
    def _():
        for g in range(NSA_GROUPS):
            kaug_scr[g, 0:s_len, 0:DH] = ks_ref[:, g * DH:(g + 1) * DH].astype(BF16)
            kaug_scr[g, 0:s_len, DH:DH + n_s] = et_ref[...]
            kaug_scr[g, s_len:s_len + SEL_TILE, 0:DH] = jnp.zeros((SEL_TILE, DH), BF16)
            kaug_scr[g, s_len:s_len + SEL_TILE, DH:DH + n_s] = jnp.ones((SEL_TILE, n_s), BF16)
            kw_scr[g, 0:WINDOW, :] = (lax.broadcasted_iota(jnp.int32, (WINDOW, LANE), 1) == DH).astype(BF16)
            kw_scr[g, WINDOW:WINDOW + s_len, 0:DH] = kw_ref[:, g * DH:(g + 1) * DH].astype(BF16)
            kw_scr[g, WINDOW:WINDOW + s_len, DH:LANE] = jnp.zeros((s_len, LANE - DH), BF16)
            vwt_ref[g, :, 0:WINDOW] = jnp.zeros((V_ROWS, WINDOW), BF16)

        def transpose_values(t, _):
            r0 = pl.multiple_of(t * QB, QB)
            for src, dst, pad in ((vs_ref, vst_ref, 0), (vw_ref, vwt_ref, WINDOW)):
                v_t = src[pl.ds(r0, QB), :].T.astype(BF16)
                for g in range(NSA_GROUPS):
                    dst[g, 0:DH, pl.ds(pad + r0, QB)] = v_t[g * DH:(g + 1) * DH]
            return 0

        lax.fori_loop(0, s_len // QB, transpose_values, 0)
        ones_row = (lax.broadcasted_iota(jnp.int32, (V_ROWS - DH, s_len), 0) == 0).astype(BF16)
        for g in range(NSA_GROUPS):
            vst_ref[g, DH:V_ROWS, :] = ones_row
            vwt_ref[g, DH:V_ROWS, WINDOW:WINDOW + s_len] = ones_row

    gates = jax.nn.sigmoid(gn_ref[...])

    blk = lax.broadcasted_iota(jnp.int32, (n_s, QB), 0)
    cur = (c * QB + lax.broadcasted_iota(jnp.int32, (n_s, QB), 1)) // SEL_BLOCK
    valid = blk <= cur
    forced = (blk == 0) | (blk == cur) | (blk == cur - 1)
    blk_f = blk.astype(F32)

    cmp_off = pl.multiple_of(n_cp - c * (QB // CMP_STRIDE), QB // CMP_STRIDE)
    n_far = jnp.maximum(c - 1, 0)
    n_w = WINDOW // QB + 1

    def normalised(acc):
        return acc[0:DH] * (1.0 / jnp.maximum(acc[DH:DH + 1], 1e-30))

    def gate_row(g, branch):
        return jnp.concatenate([gates[3 * (g * HPG + h) + branch:3 * (g * HPG + h) + branch + 1, :]
                                for h in range(HPG)], axis=1)

    near_tiles = ((n_far - 1, n_far % 2 == 1), (c - 1, c >= 1), (c, None))
    near_rows = [pl.multiple_of(t * QB if ok is None else jnp.where(ok, t * QB, s_len), QB) for t, ok in near_tiles]
    near_cols = [pl.multiple_of(jnp.maximum(t, 0) * QB, QB) for t, _ in near_tiles]
    win_flag = jnp.where(lax.broadcasted_iota(jnp.int32, (LANE - DH, HPG * QB), 0) == 0, NEG, 0.0).astype(BF16)
    w0 = pl.multiple_of(c * QB, QB)
    groups = range(NSA_GROUPS)

    qts = [jnp.concatenate([q_ref[(g * HPG + h) * DH:(g * HPG + h + 1) * DH, :] for h in range(HPG)], axis=1)
           for g in groups]
    s_cs = [jnp.dot(kcmp_ref[g], qts[g], preferred_element_type=F32) for g in groups]
    s_ws = [jnp.dot(kw_scr[g, pl.ds(w0, WINDOW + QB), :], jnp.concatenate([qts[g], win_flag], axis=0),
                    preferred_element_type=F32) for g in groups]

    qas, o_cs = [], []
    for g in groups:
        s_c = s_cs[g] + bcmp_ref[g, pl.ds(cmp_off, n_cp), :]
        e_c = jnp.exp2(s_c - jnp.maximum(jnp.max(s_c, axis=0, keepdims=True), 0.1 * NEG))
        p_c = e_c * (1.0 / jnp.maximum(jnp.sum(e_c, axis=0, keepdims=True), 1e-30))
        o_cs.append(jnp.dot(vcmpt_ref[g], p_c.astype(BF16), preferred_element_type=F32))

        p_sum = p_c[:, 0:QB] + p_c[:, QB:2 * QB] + p_c[:, 2 * QB:3 * QB] + p_c[:, 3 * QB:4 * QB]
        p_hi = p_sum.astype(BF16)
        p_lo = (p_sum - p_hi.astype(F32)).astype(BF16)
        imp = (jnp.dot(ovt_ref[...], p_hi, preferred_element_type=F32)
               + jnp.dot(ovt_ref[...], p_lo, preferred_element_type=F32))
        score = jnp.where(forced, imp + 1e4, jnp.where(valid, imp, -1e4))
        selneg = jnp.full((n_s, QB), NEG, F32)
        for _ in range(min(SEL_TOPK, n_s)):
            m = jnp.max(score, axis=0, keepdims=True)
            first = jnp.min(jnp.where(score == m, blk_f, float(n_s)), axis=0, keepdims=True)
            pick = blk_f == first
            selneg = jnp.where(pick, 0.0, selneg)
            score = jnp.where(pick, -jnp.inf, score)
        qas.append(jnp.concatenate([qts[g], jnp.concatenate([selneg.astype(BF16)] * HPG, axis=1)], axis=0))

    s_ns = [jnp.dot(jnp.concatenate([kaug_scr[g, pl.ds(r0, QB), :] for r0 in near_rows], axis=0), qas[g],
                    preferred_element_type=F32) for g in groups]

    for g in groups:
        s_w = s_ws[g] + bwin_ref[g]
        e_w = jnp.exp2(s_w - jnp.max(s_w, axis=0, keepdims=True))
        o_w = normalised(jnp.dot(vwt_ref[g, :, pl.ds(w0, WINDOW + QB)], e_w.astype(BF16),
                                 preferred_element_type=F32))
        part_scr[g] = gate_row(g, 0) * o_cs[g] + gate_row(g, 2) * o_w

    near_m = []
    for g in groups:
        s_n = s_ns[g] + bnear_ref[g]
        m_n = jnp.max(s_n, axis=0, keepdims=True)
        near_m.append(m_n)
        v_n = jnp.concatenate([vst_ref[g, :, pl.ds(r0, QB)] for r0 in near_cols], axis=1)
        near_scr[g] = jnp.dot(v_n, jnp.exp2(s_n - m_n).astype(BF16), preferred_element_type=F32)

    n_it = n_far // 2

    def scores(kt, g, dst):
        r0 = pl.multiple_of(jnp.where(kt < n_it, kt * SEL_TILE, s_len), SEL_TILE)
        dst[g] = jnp.dot(kaug_scr[g, pl.ds(r0, SEL_TILE), :], qas[g], preferred_element_type=F32)

    def values(kt, g, p_src):
        r0 = pl.multiple_of(jnp.clip(kt, 0, jnp.maximum(n_it - 1, 0)) * SEL_TILE, SEL_TILE)
        return jnp.dot(vst_ref[g, :, pl.ds(r0, SEL_TILE)], p_src[g], preferred_element_type=F32)

    def half(kt, carries, src, dst, p_src, p_dst):
        for g in groups:
            scores(kt + 1, g, dst)
        pv = [values(kt - 1, g, p_src) for g in groups]
        out = []
        for g in groups:
            m_i, acc, alpha_prev = carries[g]
            s = src[g]
            m_n = jnp.maximum(m_i, jnp.max(s, axis=0, keepdims=True))
            p_dst[g] = jnp.exp2(s - m_n).astype(BF16)
            out.append((m_n, alpha_prev * acc + pv[g], jnp.exp2(m_i - m_n)))
        return tuple(out)

    def far(j, carries):
        first = half(2 * j, carries, sa_scr, sb_scr, pb_scr, pa_scr)
        return half(2 * j + 1, first, sb_scr, sa_scr, pa_scr, pb_scr)

    for g in groups:
        pb_scr[g] = jnp.zeros((SEL_TILE, HPG * QB), BF16)
        scores(0, g, sa_scr)
    init = (jnp.full((1, HPG * QB), NEG, F32), jnp.zeros((V_ROWS, HPG * QB), F32), jnp.ones((1, HPG * QB), F32))
    n_trips = (n_it + 1) // 2
    piped = lax.fori_loop(0, n_trips, far, (init,) * NSA_GROUPS)
    carries = [(m_i, alpha_prev * acc + values(2 * n_trips - 1, g, pb_scr))
               for g, (m_i, acc, alpha_prev) in enumerate(piped)]

    for g in range(NSA_GROUPS):
        m_f, acc_f = carries[g]
        m_f = m_f + cfar_ref[g]
        m_s = jnp.maximum(m_f, near_m[g])
        acc_s = jnp.exp2(m_f - m_s) * acc_f + jnp.exp2(near_m[g] - m_s) * near_scr[g]
        o_g = part_scr[g] + gate_row(g, 1) * normalised(acc_s)
        for h in range(HPG):
            hh = g * HPG + h
            o_ref[hh * DH:(hh + 1) * DH, :] = o_g[:, h * QB:(h + 1) * QB].astype(o_ref.dtype)


def _nsa_attention(z3, qt, gnt, kcmp, vcmpt, tabs):
    B, S, _ = z3.shape
    n_qb = S // Q_BLOCK
    et, ovt, bnear, cfar, bwin, bcmp = tabs

    def tcol(rows):
        return pl.BlockSpec((rows, Q_BLOCK), lambda b, c: (0, b * n_qb + c))

    def zfull(col):
        return pl.BlockSpec((None, S, NSA_KV), lambda b, c: (b, 0, col // NSA_KV))

    def per_batch(a):
        nd = a.ndim - 1
        return pl.BlockSpec((None,) + a.shape[1:], lambda b, c: (b,) + (0,) * nd)

    def const(a):
        nd = a.ndim
        return pl.BlockSpec(a.shape, lambda b, c: (0,) * nd)

    tile = (NSA_GROUPS, SEL_TILE, NSA_HPG * Q_BLOCK)
    return pl.pallas_call(
        _nsa_body,
        grid=(B, S // Q_BLOCK),
        in_specs=[tcol(BRANCH_W), tcol(LANE),
                  zfull(COL_KS), zfull(COL_KS + NSA_KV), zfull(COL_KS + 2 * NSA_KV), zfull(COL_KS + 3 * NSA_KV),
                  per_batch(kcmp), per_batch(vcmpt),
                  const(et), const(ovt), const(bnear), const(cfar), const(bwin), const(bcmp)],
        out_specs=tcol(BRANCH_W),
        out_shape=jax.ShapeDtypeStruct((BRANCH_W, B * S), BF16),
        scratch_shapes=[pltpu.VMEM((NSA_GROUPS, S + SEL_TILE, NSA_DH + et.shape[1]), BF16),
                        pltpu.VMEM((NSA_GROUPS, WINDOW + S, LANE), BF16),
                        pltpu.VMEM((NSA_GROUPS, V_ROWS, S), BF16), pltpu.VMEM((NSA_GROUPS, V_ROWS, WINDOW + S), BF16),
                        pltpu.VMEM((NSA_GROUPS, NSA_DH, NSA_HPG * Q_BLOCK), F32),
                        pltpu.VMEM((NSA_GROUPS, V_ROWS, NSA_HPG * Q_BLOCK), F32),
                        pltpu.VMEM(tile, F32), pltpu.VMEM(tile, F32), pltpu.VMEM(tile, BF16), pltpu.VMEM(tile, BF16)],
        compiler_params=_cparams(("parallel", "arbitrary")),
        name="nsa_attention",
    )(qt, gnt, z3, z3, z3, z3, kcmp, vcmpt, et, ovt, bnear, cfar, bwin, bcmp)


RET_SEQS = 4


def _retention_body(gch_ref, q_ref, k_ref, v_ref, g_ref, cos_ref, sin_ref, decay_ref, hm_ref, xi_ref,
                    zeta_ref, gn_ref, o_ref, state_scr):
    @pl.when(pl.program_id(1) == 0)
    def _():
        state_scr[...] = jnp.zeros_like(state_scr)

    half = RET_HEADS * RET_DK // 2
    cos, sin = cos_ref[...], sin_ref[...]

    def rot(x):
        x1, x2 = x[:, :half], x[:, half:]
        return jnp.concatenate([x1 * cos - x2 * sin, x1 * sin + x2 * cos], axis=1)

    nt = (((1,), (1,)), ((), ()))
    tn = (((0,), (0,)), ((), ()))
    pairs = [(r, h) for r in range(RET_SEQS) for h in range(RET_HEADS)]
    qr = [rot(q_ref[r]) for r in range(RET_SEQS)]
    kr = [rot(k_ref[r]) for r in range(RET_SEQS)]
    kb = [x.astype(BF16) for x in kr]
    vh = {(r, h): v_ref[r, :, h * RET_DV:(h + 1) * RET_DV].astype(BF16) for r, h in pairs}
    inner = {(r, h): lax.dot_general((qr[r] * hm_ref[h]).astype(BF16), kb[r], nt, preferred_element_type=F32)
             for r, h in pairs}
    state = {(r, h): state_scr[r, h] for r, h in pairs}
    cross = {(r, h): jnp.dot((qr[r] * xi_ref[h]).astype(BF16), state[r, h].astype(BF16),
                             preferred_element_type=F32) for r, h in pairs}
    kv = {(r, h): lax.dot_general((kr[r] * zeta_ref[h]).astype(BF16), vh[r, h], tn, preferred_element_type=F32)
          for r, h in pairs}
    for r, h in pairs:
        state_scr[r, h] = gch_ref[h] * state[r, h] + kv[r, h]
    for r, h in pairs:
        cols = slice(h * RET_DV, (h + 1) * RET_DV)
        o = jnp.dot((inner[r, h] * decay_ref[h]).astype(BF16), vh[r, h], preferred_element_type=F32) + cross[r, h]
        mu = jnp.mean(o, axis=-1, keepdims=True)
        d = o - mu
        var = jnp.mean(d * d, axis=-1, keepdims=True)
        y = d * lax.rsqrt(var + EPS) * gn_ref[:, cols]
        gate = g_ref[r, :, cols]
        o_ref[r, :, cols] = (gate * jax.nn.sigmoid(gate) * y).astype(o_ref.dtype)


def _retention(z3, tabs, gn):
    B, S, _ = z3.shape
    C = RET_CHUNK
    gch, cos, sin, decay, hm, xi, zeta = tabs
    qk_w = RET_HEADS * RET_DK

    def zcol(width, col):
        return pl.BlockSpec((RET_SEQS, C, width), lambda b, n: (b, n, col // width))

    def const(a):
        nd = a.ndim
        return pl.BlockSpec(a.shape, lambda b, n: (0,) * nd)

    pos = pl.BlockSpec((C, qk_w // 2), lambda b, n: (n, 0))
    return pl.pallas_call(
        _retention_body,
        grid=(B // RET_SEQS, S // C),
        in_specs=[pl.BlockSpec(memory_space=pltpu.SMEM),
                  zcol(qk_w, COL_QR), zcol(qk_w, COL_KR), zcol(BRANCH_W, COL_VR), zcol(BRANCH_W, COL_GR),
                  pos, pos, const(decay), const(hm), const(xi), const(zeta), const(gn)],
        out_specs=pl.BlockSpec((RET_SEQS, C, BRANCH_W), lambda b, n: (b, n, 0)),
        out_shape=jax.ShapeDtypeStruct((B, S, BRANCH_W), BF16),
        scratch_shapes=[pltpu.VMEM((RET_SEQS, RET_HEADS, qk_w, RET_DV), F32)],
        compiler_params=_cparams(("parallel", "arbitrary")),
        name="retention",
    )(gch, z3, z3, z3, z3, cos, sin, decay, hm, xi, zeta, gn)


def _merge_body(x_ref, g_ref, wg_ref, a_ref, r_ref, c_ref, wb_ref, wo_ref, o_ref):
    x = x_ref[...]
    h = _rmsnorm_bf16(x, g_ref[...])
    merged = None
    nsa = a_ref[...].astype(F32).T.astype(BF16)
    for i, br in enumerate((nsa, r_ref[...], c_ref[...])):
        gate = jnp.dot(h, wg_ref[:, i * D_MODEL:(i + 1) * D_MODEL], preferred_element_type=F32)
        proj = jnp.dot(br, wb_ref[i], preferred_element_type=F32)
        term = jax.nn.sigmoid(gate) * proj
        merged = term if merged is None else merged + term
    o_ref[...] = x + jnp.dot(merged.astype(BF16), wo_ref[...], preferred_element_type=F32)


def _merge(x2, g, wg, o_nsa, o_ret, o_conv, wb, wo):
    T = x2.shape[0]
    tm = 512
    br = pl.BlockSpec((tm, BRANCH_W), lambda i: (i, 0))
    return pl.pallas_call(
        _merge_body,
        grid=(T // tm,),
        in_specs=[pl.BlockSpec((tm, D_MODEL), lambda i: (i, 0)),
                  pl.BlockSpec((1, D_MODEL), lambda i: (0, 0)),
                  pl.BlockSpec(wg.shape, lambda i: (0, 0)),
                  pl.BlockSpec((BRANCH_W, tm), lambda i: (0, i)), br, br,
                  pl.BlockSpec(wb.shape, lambda i: (0, 0, 0)),
                  pl.BlockSpec(wo.shape, lambda i: (0, 0))],
        out_specs=pl.BlockSpec((tm, D_MODEL), lambda i: (i, 0)),
        out_shape=jax.ShapeDtypeStruct((T, D_MODEL), F32),
        compiler_params=_cparams(("parallel",)),
        name="merge_out_proj",
    )(x2, g, wg, o_nsa, o_ret, o_conv, wb, wo)


FF_CHUNK = 1024


def _ffn_body(x_ref, g_ref, w1_ref, w2_ref, gf_ref, o_ref, *, final_norm):
    x = x_ref[...]
    ms = jnp.mean(x * x, axis=-1, keepdims=True)
    h = (x * lax.rsqrt(ms + EPS) * g_ref[...]).astype(BF16)
    acc = x
    for j in range(D_FF // FF_CHUNK):
        cols = slice(j * FF_CHUNK, (j + 1) * FF_CHUNK)
        a = jnp.maximum(jnp.dot(h, w1_ref[:, cols], preferred_element_type=F32), 0.0)
        acc = acc + jnp.dot((a * a).astype(BF16), w2_ref[cols, :], preferred_element_type=F32)
    if final_norm:
        ms = jnp.mean(acc * acc, axis=-1, keepdims=True)
        acc = acc * lax.rsqrt(ms + EPS) * gf_ref[...]
    o_ref[...] = acc


def _ffn(x2, g, w1, w2, gf, final_norm):
    T = x2.shape[0]
    tm = 512
    vec = pl.BlockSpec((1, D_MODEL), lambda i: (0, 0))
    return pl.pallas_call(
        functools.partial(_ffn_body, final_norm=final_norm),
        grid=(T // tm,),
        in_specs=[pl.BlockSpec((tm, D_MODEL), lambda i: (i, 0)), vec,
                  pl.BlockSpec(w1.shape, lambda i: (0, 0)),
                  pl.BlockSpec(w2.shape, lambda i: (0, 0)), vec],
        out_specs=pl.BlockSpec((tm, D_MODEL), lambda i: (i, 0)),
        out_shape=jax.ShapeDtypeStruct((T, D_MODEL), F32),
        compiler_params=_cparams(("parallel",)),
        name="ffn",
    )(x2, g, w1, w2, gf)


def _rel_bucket(dist):
    n = jnp.maximum(dist, 0)
    max_exact = REL_BUCKETS // 2
    nf = jnp.maximum(n, 1).astype(F32)
    large = max_exact + (jnp.log(nf / max_exact) / math.log(REL_MAX_DIST / max_exact)
                         * (REL_BUCKETS - max_exact)).astype(jnp.int32)
    large = jnp.minimum(large, REL_BUCKETS - 1)
    return jnp.where(n < max_exact, n, large)


def _nsa_tables(rel_table, S):
    n_s = S // SEL_BLOCK
    n_cp = S // CMP_STRIDE
    QB, G, HPG = Q_BLOCK, NSA_GROUPS, NSA_HPG
    tab = rel_table.T.astype(F32) * LOG2E

    def bias(dist, ok=None):
        bucket = _rel_bucket(dist)[None]
        full = jnp.zeros((NSA_HEADS,) + dist.shape, F32)
        for b in range(REL_BUCKETS):
            full = jnp.where(bucket == b, tab[:, b][:, None, None], full)
        if ok is not None:
            full = jnp.where(ok[None], full, NEG)
        J = dist.shape[0]
        return jnp.transpose(full.reshape(G, HPG, J, QB), (0, 2, 1, 3)).reshape(G, J, HPG * QB)

    i = jnp.arange(QB)[None, :]
    et = (jnp.arange(S)[:, None] // SEL_BLOCK == jnp.arange(n_s)[None, :]).astype(BF16)
    cmp_start = jnp.arange(n_cp) * CMP_STRIDE
    sel_start = jnp.arange(n_s) * SEL_BLOCK
    ovt = ((cmp_start[None, :] <= sel_start[:, None] + SEL_BLOCK - 1)
           & (cmp_start[None, :] + CMP_BLOCK - 1 >= sel_start[:, None])
           & (jnp.arange(n_cp)[None, :] < n_cp - 1)).astype(BF16)
    d_near = i + (NEAR_TILES - 1) * QB - jnp.arange(NEAR_TILES * QB)[:, None]
    bnear = bias(d_near, d_near >= 0)
    cfar = bias(jnp.full((1, QB), S + REL_MAX_DIST, jnp.int32))
    d_win = i + WINDOW - jnp.arange(WINDOW + QB)[:, None]
    bwin = bias(d_win, (d_win >= 0) & (d_win < WINDOW))
    d_cmp = i - CMP_STRIDE * (jnp.arange(2 * n_cp)[:, None] - n_cp) - (CMP_BLOCK - 1)
    bcmp = bias(d_cmp, d_cmp >= 0)
    return et, ovt, bnear, cfar, bwin, bcmp


def _retention_tables(S):
    H, DK, C = RET_HEADS, RET_DK, RET_CHUNK
    half = DK // 2
    pos = jnp.arange(S, dtype=F32)
    inv = ROPE_BASE ** (-jnp.arange(half, dtype=F32) / half)
    ang = pos[:, None] * inv[None, :]
    cos = jnp.tile(jnp.cos(ang), (1, H))
    sin = jnp.tile(jnp.sin(ang), (1, H))
    log_g = jnp.log1p(-jnp.exp2(-5.0 - jnp.arange(H, dtype=F32)))
    ar = jnp.arange(C)
    diff = ar[:, None] - ar[None, :]
    decay = jnp.where(diff >= 0, jnp.exp(log_g[:, None, None] * jnp.maximum(diff, 0).astype(F32)), 0.0)
    zeta = jnp.exp(log_g[:, None] * (C - 1 - ar).astype(F32))
    xi = jnp.exp(log_g[:, None] * (ar + 1).astype(F32))
    gch = jnp.exp(log_g * C)
    lane_head = (jnp.arange(H * DK) % (H * half)) // half
    hm = (lane_head[None, :] == jnp.arange(H)[:, None]).astype(F32)
    return (gch, cos, sin, decay, hm[:, None, :], xi[:, :, None] * hm[:, None, :],
            zeta[:, :, None] * hm[:, None, :])


def _pack_w_in(w):
    o = np.cumsum([0, 512, 128, 128, 128, 128, 128, 128, 24, 256, 256, 512, 512, 512, 512, 3072])
    seg = lambda i: w[:, int(o[i]):int(o[i + 1])]

    def halves_major(a):
        a4 = a.reshape(a.shape[0], RET_HEADS, 2, RET_DK // 2)
        return jnp.transpose(a4, (0, 2, 1, 3)).reshape(a.shape[0], RET_HEADS * RET_DK)

    parts = [seg(12), seg(13), seg(0) * (NSA_DH ** -0.5 * LOG2E),
             seg(7), jnp.zeros((w.shape[0], LANE - 3 * NSA_HEADS), w.dtype), seg(10), seg(11),
             halves_major(seg(8)), halves_major(seg(9)) * (RET_DK ** -0.5),
             seg(3), seg(4), seg(5), seg(6), seg(1), seg(2)]
    return jnp.concatenate(parts, axis=1).astype(BF16), seg(14).astype(BF16)


def _pack_compress(pe, w1, w2):
    G, DH, r = NSA_GROUPS, NSA_DH, CMP_BLOCK // CMP_STRIDE
    eye = jnp.eye(G, dtype=w1.dtype)
    pe2 = jnp.tile(pe.reshape(r, CMP_STRIDE, 1, DH), (1, 1, G, 1)).reshape(r, CMP_STRIDE * G * DH)
    w1r = w1.reshape(r, CMP_STRIDE, DH, CMP_HIDDEN)
    w1b = jnp.einsum('rldf,gk->rlgdkf', w1r, eye).reshape(r, CMP_STRIDE * G * DH, G * CMP_HIDDEN)
    w2b = jnp.einsum('fd,gk->gfkd', w2, eye).reshape(G * CMP_HIDDEN, G * DH)
    return pe2, w1b.astype(BF16), w2b.astype(BF16)


def kernel(x, rel_table, norm_mix, w_in, cmp_pe_k, cmp_w1_k, cmp_w2_k, cmp_pe_v, cmp_w1_v, cmp_w2_v, ret_gn, conv_w, conv_b, conv_ln_g, conv_ln_b, w_branch, w_out, norm_mlp, w_ff1, w_ff2, norm_final):
    B, S, D = x.shape
    assert B % RET_SEQS == 0 and S % (2 * SEL_TILE) == 0
    depth = w_in.shape[0]
    T = B * S
    nsa_tabs = _nsa_tables(rel_table, S)
    ret_tabs = _retention_tables(S)
    x2 = x.reshape(T, D)
    for l in range(depth):
        w_main, w_gate = _pack_w_in(w_in[l])
        conv_w_pad = jnp.concatenate([conv_w[l], jnp.zeros((1, BRANCH_W), conv_w.dtype)], axis=0)
        z2, kc, vc, o_conv, qt, gnt = _in_proj(x2, norm_mix[l][None, :], w_main, S, conv_w_pad, conv_b[l][None, :],
                                      conv_ln_g[l][None, :], conv_ln_b[l][None, :])
        z3 = z2.reshape(B, S, Z_COLS)
        n_ch = S // CMP_STRIDE
        kc2 = kc.reshape(B, n_ch, CMP_STRIDE * NSA_KV)
        vc2 = vc.reshape(B, n_ch, CMP_STRIDE * NSA_KV)
        kcmp, vcmp = _compress(kc2, vc2, *_pack_compress(cmp_pe_k[l], cmp_w1_k[l], cmp_w2_k[l]),
                               *_pack_compress(cmp_pe_v[l], cmp_w1_v[l], cmp_w2_v[l]))
        kcmp = jnp.transpose(kcmp.reshape(B, n_ch, NSA_GROUPS, NSA_DH), (0, 2, 1, 3))
        vcmpt = jnp.transpose(vcmp.reshape(B, n_ch, NSA_GROUPS, NSA_DH), (0, 2, 3, 1))

        o_nsa_t = _nsa_attention(z3, qt, gnt, kcmp, vcmpt, nsa_tabs)
        o_ret = _retention(z3, ret_tabs, ret_gn[l][None, :])
        x2 = _merge(x2, norm_mix[l][None, :], w_gate, o_nsa_t, o_ret.reshape(T, BRANCH_W), o_conv.reshape(T, BRANCH_W),
                    w_branch[l].astype(BF16), w_out[l].astype(BF16))
        x2 = _ffn(x2, norm_mlp[l][None, :], w_ff1[l].astype(BF16), w_ff2[l].astype(BF16), norm_final[None, :],
                  final_norm=(l == depth - 1))
    return x2.reshape(B, S, D)
```

```python
import functools
import math

import jax
import jax.numpy as jnp
import numpy as np
from jax import lax
from jax.experimental import pallas as pl
from jax.experimental.pallas import tpu as pltpu

F32 = jnp.float32
BF16 = jnp.bfloat16

D_MODEL = 1024
BRANCH_W = 512
N_BRANCH = 3
NSA_HEADS = 8
NSA_GROUPS = 2
NSA_HPG = 4
NSA_DH = 64
NSA_KV = NSA_GROUPS * NSA_DH
CMP_BLOCK = 32
CMP_STRIDE = 16
CMP_HIDDEN = 128
SEL_BLOCK = 64
SEL_TOPK = 8
WINDOW = 512
Q_BLOCK = 128
RET_HEADS = 4
RET_DV = 128
RET_DK = 64
RET_CHUNK = 128
ROPE_BASE = 10000.0
CONV_WIDTH = 31
REL_BUCKETS = 32
REL_MAX_DIST = 128
D_FF = 4096
EPS = 1e-6
NEG = -1e30
LOG2E = 1.4426950408889634

LANE = 128
VMEM_LIMIT = 56 * 1024 * 1024

GLU_COLS = 2 * BRANCH_W
QT_COLS = BRANCH_W + LANE
COL_VR = 0
COL_GR = 512
COL_QR = 1024
COL_KR = 1280
COL_KS = 1536
Z_COLS = 2048
W_COLS = GLU_COLS + QT_COLS + Z_COLS + 2 * NSA_KV


def _cparams(sem):
    return pltpu.CompilerParams(dimension_semantics=sem, vmem_limit_bytes=VMEM_LIMIT)


def _rmsnorm_bf16(x, g):
    ms = jnp.mean(x * x, axis=-1, keepdims=True)
    return (x * lax.rsqrt(ms + EPS) * g).astype(BF16)


CONV_HALO = 32
SUBLANES = 8
CONV_ROWS = 64


def _conv_stage(u, u_scr):
    tile = u.shape[0]
    rows = CONV_HALO + tile
    u_scr[0, 0:CONV_HALO, :] = u_scr[0, tile:rows, :]
    u_scr[0, CONV_HALO:rows, :] = u
    for r in range(1, SUBLANES):
        u_scr[r, 0:rows - SUBLANES, :] = u_scr[0, r:rows - SUBLANES + r, :]


def _conv_rows(r0, w_ref, cb_ref, lg_ref, lb_ref, o_ref, u_scr):
    lead = CONV_HALO - (CONV_WIDTH - 1)
    acc = jnp.zeros((CONV_ROWS, u_scr.shape[2]), F32) + cb_ref[...]
    for k in range(CONV_WIDTH):
        start = r0 + (lead + k) // SUBLANES * SUBLANES
        acc = acc + w_ref[k:k + 1, :] * u_scr[(lead + k) % SUBLANES, pl.ds(start, CONV_ROWS), :]
    mu = jnp.mean(acc, axis=-1, keepdims=True)
    d = acc - mu
    var = jnp.mean(d * d, axis=-1, keepdims=True)
    y = d * lax.rsqrt(var + EPS) * lg_ref[...] + lb_ref[...]
    o_ref[pl.ds(r0, CONV_ROWS), :] = (y * jax.nn.sigmoid(y)).astype(o_ref.dtype)


def _in_proj_body(x_ref, g_ref, w_ref, cw_ref, cb_ref, lg_ref, lb_ref, z_ref, kc_ref, vc_ref, oc_ref, qt_ref, gnt_ref,
                  u_scr, *, tiles_per_seq):
    tm = x_ref.shape[0]

    @pl.when(pl.program_id(0) % tiles_per_seq == 0)
    def _():
        u_scr[0, tm:tm + CONV_HALO, :] = jnp.zeros((CONV_HALO, u_scr.shape[2]), F32)

    h = _rmsnorm_bf16(x_ref[...], g_ref[...])
    glu = jnp.dot(h, w_ref[:, 0:GLU_COLS], preferred_element_type=F32)
    acc = jnp.dot(h, w_ref[:, GLU_COLS:W_COLS], preferred_element_type=F32)
    qt_ref[...] = acc[:, 0:BRANCH_W].T.astype(qt_ref.dtype)
    gnt_ref[...] = acc[:, BRANCH_W:QT_COLS].T
    z_ref[...] = acc[:, QT_COLS:QT_COLS + Z_COLS]
    kc_ref[...] = acc[:, QT_COLS + Z_COLS:QT_COLS + Z_COLS + NSA_KV]
    vc_ref[...] = acc[:, QT_COLS + Z_COLS + NSA_KV:QT_COLS + Z_COLS + 2 * NSA_KV]
    _conv_stage(glu[:, 0:BRANCH_W] * jax.nn.sigmoid(glu[:, BRANCH_W:GLU_COLS]), u_scr)
    for r0 in range(0, tm, CONV_ROWS):
        _conv_rows(r0, cw_ref, cb_ref, lg_ref, lb_ref, oc_ref, u_scr)


def _in_proj(x2, g, w, seq_len, conv_w, conv_b, conv_lg, conv_lb):
    T = x2.shape[0]
    tm = 512
    assert seq_len % tm == 0

    def const(a):
        return pl.BlockSpec(a.shape, lambda i: (0, 0))

    def rows(width):
        return pl.BlockSpec((tm, width), lambda i: (i, 0))

    return pl.pallas_call(
        functools.partial(_in_proj_body, tiles_per_seq=seq_len // tm),
        grid=(T // tm,),
        in_specs=[rows(D_MODEL), const(g), const(w), const(conv_w), const(conv_b), const(conv_lg), const(conv_lb)],
        out_specs=[rows(Z_COLS), rows(NSA_KV), rows(NSA_KV), rows(BRANCH_W),
                   pl.BlockSpec((BRANCH_W, tm), lambda i: (0, i)), pl.BlockSpec((LANE, tm), lambda i: (0, i))],
        out_shape=[jax.ShapeDtypeStruct((T, Z_COLS), F32), jax.ShapeDtypeStruct((T, NSA_KV), F32),
                   jax.ShapeDtypeStruct((T, NSA_KV), F32), jax.ShapeDtypeStruct((T, BRANCH_W), BF16),
                   jax.ShapeDtypeStruct((BRANCH_W, T), BF16), jax.ShapeDtypeStruct((LANE, T), F32)],
        scratch_shapes=[pltpu.VMEM((SUBLANES, CONV_HALO + tm, BRANCH_W), F32)],
        compiler_params=_cparams(("arbitrary",)),
        name="in_proj_conv",
    )(x2, g, w, conv_w, conv_b, conv_lg, conv_lb)


def _compress_body(kc_ref, vc_ref, pek_ref, w1k_ref, w2k_ref, pev_ref, w1v_ref, w2v_ref, ko_ref, vo_ref):
    def one(c_ref, pe_ref, w1_ref, w2_ref, o_ref):
        ch = c_ref[...]
        a = jnp.dot((ch + pe_ref[0:1, :]).astype(BF16), w1_ref[0], preferred_element_type=F32)
        b = jnp.dot((ch + pe_ref[1:2, :]).astype(BF16), w1_ref[1], preferred_element_type=F32)
        n = b.shape[0]
        hid = jax.nn.gelu(a + pltpu.roll(b, n - 1, 0))
        o_ref[...] = jnp.dot(hid.astype(BF16), w2_ref[...], preferred_element_type=F32).astype(o_ref.dtype)

    one(kc_ref, pek_ref, w1k_ref, w2k_ref, ko_ref)
    one(vc_ref, pev_ref, w1v_ref, w2v_ref, vo_ref)


def _compress(kc2, vc2, pek, w1k, w2k, pev, w1v, w2v):
    B, n_ch, wide = kc2.shape
    hid2 = NSA_GROUPS * CMP_HIDDEN
    act = pl.BlockSpec((None, n_ch, wide), lambda b: (b, 0, 0))
    pe = pl.BlockSpec((2, wide), lambda b: (0, 0))
    w1 = pl.BlockSpec((2, wide, hid2), lambda b: (0, 0, 0))
    w2 = pl.BlockSpec((hid2, NSA_KV), lambda b: (0, 0))
    out = pl.BlockSpec((None, n_ch, NSA_KV), lambda b: (b, 0, 0))
    return pl.pallas_call(
        _compress_body,
        grid=(B,),
        in_specs=[act, act, pe, w1, w2, pe, w1, w2],
        out_specs=[out, out],
        out_shape=[jax.ShapeDtypeStruct((B, n_ch, NSA_KV), BF16)] * 2,
        compiler_params=_cparams(("parallel",)),
        name="nsa_compress",
    )(kc2, vc2, pek, w1k, w2k, pev, w1v, w2v)


SEL_TILE = 2 * Q_BLOCK
V_ROWS = NSA_DH + 16
NEAR_TILES = 3


def _nsa_body(q_ref, gn_ref, ks_ref, vs_ref, kw_ref, vw_ref, kcmp_ref, vcmpt_ref,
              et_ref, ovt_ref, bnear_ref, cfar_ref, bwin_ref, bcmp_ref, o_ref,
              kaug_scr, kw_scr, vst_ref, vwt_ref, part_scr, near_scr, sa_scr, sb_scr, pa_scr, pb_scr):
    c = pl.program_id(1)
    n_s = et_ref.shape[1]
    n_cp = kcmp_ref.shape[1]
    QB, DH, HPG = Q_BLOCK, NSA_DH, NSA_HPG
    s_len = ks_ref.shape[0]

    @pl.when(c == 0)
    def _():
        for g in range(NSA_GROUPS):
            kaug_scr[g, 0:s_len, 0:DH] = ks_ref[:, g * DH:(g + 1) * DH].astype(BF16)
            kaug_scr[g, 0:s_len, DH:DH + n_s] = et_ref[...]
            kaug_scr[g, s_len:s_len + SEL_TILE, 0:DH] = jnp.zeros((SEL_TILE, DH), BF16)
            kaug_scr[g, s_len:s_len + SEL_TILE, DH:DH + n_s] = jnp.ones((SEL_TILE, n_s), BF16)
            kw_scr[g, 0:WINDOW, :] = (lax.broadcasted_iota(jnp.int32, (WINDOW, LANE), 1) == DH).astype(BF16)
            kw_scr[g, WINDOW:WINDOW + s_len, 0:DH] = kw_ref[:, g * DH:(g + 1) * DH].astype(BF16)
            kw_scr[g, WINDOW:WINDOW + s_len, DH:LANE] = jnp.zeros((s_len, LANE - DH), BF16)
            vwt_ref[g, :, 0:WINDOW] = jnp.zeros((V_ROWS, WINDOW), BF16)

        def transpose_values(t, _):
            r0 = pl.multiple_of(t * QB, QB)
            for src, dst, pad in ((vs_ref, vst_ref, 0), (vw_ref, vwt_ref, WINDOW)):
                v_t = src[pl.ds(r0, QB), :].T.astype(BF16)
                for g in range(NSA_GROUPS):
                    dst[g, 0:DH, pl.ds(pad + r0, QB)] = v_t[g * DH:(g + 1) * DH]
            return 0

        lax.fori_loop(0, s_len // QB, transpose_values, 0)
        ones_row = (lax.broadcasted_iota(jnp.int32, (V_ROWS - DH, s_len), 0) == 0).astype(BF16)
        for g in range(NSA_GROUPS):
            vst_ref[g, DH:V_ROWS, :] = ones_row
            vwt_ref[g, DH:V_ROWS, WINDOW:WINDOW + s_len] = ones_row

    gates = jax.nn.sigmoid(gn_ref[...])

    blk = lax.broadcasted_iota(jnp.int32, (n_s, QB), 0)
    cur = (c * QB + lax.broadcasted_iota(jnp.int32, (n_s, QB), 1)) // SEL_BLOCK
    valid = blk <= cur
    forced = (blk == 0) | (blk == cur) | (blk == cur - 1)
    blk_f = blk.astype(F32)

    cmp_off = pl.multiple_of(n_cp - c * (QB // CMP_STRIDE), QB // CMP_STRIDE)
    n_far = jnp.maximum(c - 1, 0)
    n_w = WINDOW // QB + 1

    def normalised(acc):
        return acc[0:DH] * (1.0 / jnp.maximum(acc[DH:DH + 1], 1e-30))

    def gate_row(g, branch):
        return jnp.concatenate([gates[3 * (g * HPG + h) + branch:3 * (g * HPG + h) + branch + 1, :]
                                for h in range(HPG)], axis=1)

    near_tiles = ((n_far - 1, n_far % 2 == 1), (c - 1, c >= 1), (c, None))
    near_rows = [pl.multiple_of(t * QB if ok is None else jnp.where(ok, t * QB, s_len), QB) for t, ok in near_tiles]
    near_cols = [pl.multiple_of(jnp.maximum(t, 0) * QB, QB) for t, _ in near_tiles]
    win_flag = jnp.where(lax.broadcasted_iota(jnp.int32, (LANE - DH, HPG * QB), 0) == 0, NEG, 0.0).astype(BF16)
    w0 = pl.multiple_of(c * QB, QB)
    groups = range(NSA_GROUPS)

    qts = [jnp.concatenate([q_ref[(g * HPG + h) * DH:(g * HPG + h + 1) * DH, :] for h in range(HPG)], axis=1)
           for g in groups]
    s_cs = [jnp.dot(kcmp_ref[g], qts[g], preferred_element_type=F32) for g in groups]
    s_ws = [jnp.dot(kw_scr[g, pl.ds(w0, WINDOW + QB), :], jnp.concatenate([qts[g], win_flag], axis=0),
                    preferred_element_type=F32) for g in groups]

    qas, o_cs = [], []
    for g in groups:
        s_c = s_cs[g] + bcmp_ref[g, pl.ds(cmp_off, n_cp), :]
        e_c = jnp.exp2(s_c - jnp.maximum(jnp.max(s_c, axis=0, keepdims=True), 0.1 * NEG))
        p_c = e_c * (1.0 / jnp.maximum(jnp.sum(e_c, axis=0, keepdims=True), 1e-30))
        o_cs.append(jnp.dot(vcmpt_ref[g], p_c.astype(BF16), preferred_element_type=F32))

        p_sum = p_c[:, 0:QB] + p_c[:, QB:2 * QB] + p_c[:, 2 * QB:3 * QB] + p_c[:, 3 * QB:4 * QB]
        p_hi = p_sum.astype(BF16)
        p_lo = (p_sum - p_hi.astype(F32)).astype(BF16)
        imp = (jnp.dot(ovt_ref[...], p_hi, preferred_element_type=F32)
               + jnp.dot(ovt_ref[...], p_lo, preferred_element_type=F32))
        score = jnp.where(forced, imp + 1e4, jnp.where(valid, imp, -1e4))
        selneg = jnp.full((n_s, QB), NEG, F32)
        for _ in range(min(SEL_TOPK, n_s)):
            m = jnp.max(score, axis=0, keepdims=True)
            first = jnp.min(jnp.where(score == m, blk_f, float(n_s)), axis=0, keepdims=True)
            pick = blk_f == first
            selneg = jnp.where(pick, 0.0, selneg)
            score = jnp.where(pick, -jnp.inf, score)
        qas.append(jnp.concatenate([qts[g], jnp.concatenate([selneg.astype(BF16)] * HPG, axis=1)], axis=0))

    s_ns = [jnp.dot(jnp.concatenate([kaug_scr[g, pl.ds(r0, QB), :] for r0 in near_rows], axis=0), qas[g],
                    preferred_element_type=F32) for g in groups]

    for g in groups:
        s_w = s_ws[g] + bwin_ref[g]
        e_w = jnp.exp2(s_w - jnp.max(s_w, axis=0, keepdims=True))
        o_w = normalised(jnp.dot(vwt_ref[g, :, pl.ds(w0, WINDOW + QB)], e_w.astype(BF16),
                                 preferred_element_type=F32))
        part_scr[g] = gate_row(g, 0) * o_cs[g] + gate_row(g, 2) * o_w

    near_m = []
    for g in groups:
        s_n = s_ns[g] + bnear_ref[g]
        m_n = jnp.max(s_n, axis=0, keepdims=True)
        near_m.append(m_n)
        v_n = jnp.concatenate([vst_ref[g, :, pl.ds(r0, QB)] for r0 in near_cols], axis=1)
        near_scr[g] = jnp.dot(v_n, jnp.exp2(s_n - m_n).astype(BF16), preferred_element_type=F32)

    n_it = n_far // 2

    def scores(kt, g, dst):
        r0 = pl.multiple_of(jnp.where(kt < n_it, kt * SEL_TILE, s_len), SEL_TILE)
        dst[g] = jnp.dot(kaug_scr[g, pl.ds(r0, SEL_TILE), :], qas[g], preferred_element_type=F32)

    def values(kt, g, p_src):
        r0 = pl.multiple_of(jnp.clip(kt, 0, jnp.maximum(n_it - 1, 0)) * SEL_TILE, SEL_TILE)
        return jnp.dot(vst_ref[g, :, pl.ds(r0, SEL_TILE)], p_src[g], preferred_element_type=F32)

    def half(kt, carries, src, dst, p_src, p_dst):
        if dst is not None:
            for g in groups:
                scores(kt + 1, g, dst)
        pv = [values(kt - 1, g, p_src) for g in groups]
        out = []
        for g in groups:
            m_i, acc, alpha_prev = carries[g]
            s = src[g]
            m_n = jnp.maximum(m_i, jnp.max(s, axis=0, keepdims=True))
            p_dst[g] = jnp.exp2(s - m_n).astype(BF16)
            out.append((m_n, alpha_prev * acc + pv[g], jnp.exp2(m_i - m_n)))
        return tuple(out)

    def far(j, carries):
        first = half(2 * j, carries, sa_scr, sb_scr, pb_scr, pa_scr)
        return half(2 * j + 1, first, sb_scr, sa_scr, pa_scr, pb_scr)

    for g in groups:
        pb_scr[g] = jnp.zeros((SEL_TILE, HPG * QB), BF16)
        scores(0, g, sa_scr)
    init = (jnp.full((1, HPG * QB), NEG, F32), jnp.zeros((V_ROWS, HPG * QB), F32), jnp.ones((1, HPG * QB), F32))
    n_trips = n_it // 2
    piped = lax.fori_loop(0, n_trips, far, (init,) * NSA_GROUPS)

    def drain(kt, p_src, state):
        return tuple((m_i, alpha_prev * acc + values(kt, g, p_src)) for g, (m_i, acc, alpha_prev) in enumerate(state))

    def odd_tail(state):
        return drain(2 * n_trips, pa_scr, half(2 * n_trips, state, sa_scr, None, pb_scr, pa_scr))

    def even_tail(state):
        return drain(2 * n_trips - 1, pb_scr, state)

    carries = lax.cond(n_it % 2 == 1, odd_tail, even_tail, piped)

    for g in range(NSA_GROUPS):
        m_f, acc_f = carries[g]
        m_f = m_f + cfar_ref[g]
        m_s = jnp.maximum(m_f, near_m[g])
        acc_s = jnp.exp2(m_f - m_s) * acc_f + jnp.exp2(near_m[g] - m_s) * near_scr[g]
        o_g = part_scr[g] + gate_row(g, 1) * normalised(acc_s)
        for h in range(HPG):
            hh = g * HPG + h
            o_ref[hh * DH:(hh + 1) * DH, :] = o_g[:, h * QB:(h + 1) * QB].astype(o_ref.dtype)


def _nsa_attention(z3, qt, gnt, kcmp, vcmpt, tabs):
    B, S, _ = z3.shape
    n_qb = S // Q_BLOCK
    et, ovt, bnear, cfar, bwin, bcmp = tabs

    def tcol(rows):
        return pl.BlockSpec((rows, Q_BLOCK), lambda b, c: (0, b * n_qb + c))

    def zfull(col):
        return pl.BlockSpec((None, S, NSA_KV), lambda b, c: (b, 0, col // NSA_KV))

    def per_batch(a):
        nd = a.ndim - 1
        return pl.BlockSpec((None,) + a.shape[1:], lambda b, c: (b,) + (0,) * nd)

    def const(a):
        nd = a.ndim
        return pl.BlockSpec(a.shape, lambda b, c: (0,) * nd)

    tile = (NSA_GROUPS, SEL_TILE, NSA_HPG * Q_BLOCK)
    return pl.pallas_call(
        _nsa_body,
        grid=(B, S // Q_BLOCK),
        in_specs=[tcol(BRANCH_W), tcol(LANE),
                  zfull(COL_KS), zfull(COL_KS + NSA_KV), zfull(COL_KS + 2 * NSA_KV), zfull(COL_KS + 3 * NSA_KV),
                  per_batch(kcmp), per_batch(vcmpt),
                  const(et), const(ovt), const(bnear), const(cfar), const(bwin), const(bcmp)],
        out_specs=tcol(BRANCH_W),
        out_shape=jax.ShapeDtypeStruct((BRANCH_W, B * S), BF16),
        scratch_shapes=[pltpu.VMEM((NSA_GROUPS, S + SEL_TILE, NSA_DH + et.shape[1]), BF16),
                        pltpu.VMEM((NSA_GROUPS, WINDOW + S, LANE), BF16),
                        pltpu.VMEM((NSA_GROUPS, V_ROWS, S), BF16), pltpu.VMEM((NSA_GROUPS, V_ROWS, WINDOW + S), BF16),
                        pltpu.VMEM((NSA_GROUPS, NSA_DH, NSA_HPG * Q_BLOCK), F32),
                        pltpu.VMEM((NSA_GROUPS, V_ROWS, NSA_HPG * Q_BLOCK), F32),
                        pltpu.VMEM(tile, F32), pltpu.VMEM(tile, F32), pltpu.VMEM(tile, BF16), pltpu.VMEM(tile, BF16)],
        compiler_params=_cparams(("parallel", "arbitrary")),
        name="nsa_attention",
    )(qt, gnt, z3, z3, z3, z3, kcmp, vcmpt, et, ovt, bnear, cfar, bwin, bcmp)


RET_SEQS = 4


def _retention_body(gch_ref, q_ref, k_ref, v_ref, g_ref, cos_ref, sin_ref, decay_ref, hm_ref, xi_ref,
                    zeta_ref, gn_ref, o_ref, state_scr):
    @pl.when(pl.program_id(1) == 0)
    def _():
        state_scr[...] = jnp.zeros_like(state_scr)

    half = RET_HEADS * RET_DK // 2
    cos, sin = cos_ref[...], sin_ref[...]

    def rot(x):
        x1, x2 = x[:, :half], x[:, half:]
        return jnp.concatenate([x1 * cos - x2 * sin, x1 * sin + x2 * cos], axis=1)

    nt = (((1,), (1,)), ((), ()))
    tn = (((0,), (0,)), ((), ()))
    pairs = [(r, h) for r in range(RET_SEQS) for h in range(RET_HEADS)]
    qr = [rot(q_ref[r]) for r in range(RET_SEQS)]
    kr = [rot(k_ref[r]) for r in range(RET_SEQS)]
    kb = [x.astype(BF16) for x in kr]
    vh = {(r, h): v_ref[r, :, h * RET_DV:(h + 1) * RET_DV].astype(BF16) for r, h in pairs}
    inner = {(r, h): lax.dot_general((qr[r] * hm_ref[h]).astype(BF16), kb[r], nt, preferred_element_type=F32)
             for r, h in pairs}
    state = {(r, h): state_scr[r, h] for r, h in pairs}
    cross = {(r, h): jnp.dot((qr[r] * xi_ref[h]).astype(BF16), state[r, h].astype(BF16),
                             preferred_element_type=F32) for r, h in pairs}
    kv = {(r, h): lax.dot_general((kr[r] * zeta_ref[h]).astype(BF16), vh[r, h], tn, preferred_element_type=F32)
          for r, h in pairs}
    for r, h in pairs:
        state_scr[r, h] = gch_ref[h] * state[r, h] + kv[r, h]
    for r, h in pairs:
        cols = slice(h * RET_DV, (h + 1) * RET_DV)
        o = jnp.dot((inner[r, h] * decay_ref[h]).astype(BF16), vh[r, h], preferred_element_type=F32) + cross[r, h]
        mu = jnp.mean(o, axis=-1, keepdims=True)
        d = o - mu
        var = jnp.mean(d * d, axis=-1, keepdims=True)
        y = d * lax.rsqrt(var + EPS) * gn_ref[:, cols]
        gate = g_ref[r, :, cols]
        o_ref[r, :, cols] = (gate * jax.nn.sigmoid(gate) * y).astype(o_ref.dtype)


def _retention(z3, tabs, gn):
    B, S, _ = z3.shape
    C = RET_CHUNK
    gch, cos, sin, decay, hm, xi, zeta = tabs
    qk_w = RET_HEADS * RET_DK

    def zcol(width, col):
        return pl.BlockSpec((RET_SEQS, C, width), lambda b, n: (b, n, col // width))

    def const(a):
        nd = a.ndim
        return pl.BlockSpec(a.shape, lambda b, n: (0,) * nd)

    pos = pl.BlockSpec((C, qk_w // 2), lambda b, n: (n, 0))
    return pl.pallas_call(
        _retention_body,
        grid=(B // RET_SEQS, S // C),
        in_specs=[pl.BlockSpec(memory_space=pltpu.SMEM),
                  zcol(qk_w, COL_QR), zcol(qk_w, COL_KR), zcol(BRANCH_W, COL_VR), zcol(BRANCH_W, COL_GR),
                  pos, pos, const(decay), const(hm), const(xi), const(zeta), const(gn)],
        out_specs=pl.BlockSpec((RET_SEQS, C, BRANCH_W), lambda b, n: (b, n, 0)),
        out_shape=jax.ShapeDtypeStruct((B, S, BRANCH_W), BF16),
        scratch_shapes=[pltpu.VMEM((RET_SEQS, RET_HEADS, qk_w, RET_DV), F32)],
        compiler_params=_cparams(("parallel", "arbitrary")),
        name="retention",
    )(gch, z3, z3, z3, z3, cos, sin, decay, hm, xi, zeta, gn)


def _merge_body(x_ref, g_ref, wg_ref, a_ref, r_ref, c_ref, wb_ref, wo_ref, o_ref):
    x = x_ref[...]
    h = _rmsnorm_bf16(x, g_ref[...])
    merged = None
    nsa = a_ref[...].astype(F32).T.astype(BF16)
    for i, br in enumerate((nsa, r_ref[...], c_ref[...])):
        gate = jnp.dot(h, wg_ref[:, i * D_MODEL:(i + 1) * D_MODEL], preferred_element_type=F32)
        proj = jnp.dot(br, wb_ref[i], preferred_element_type=F32)
        term = jax.nn.sigmoid(gate) * proj
        merged = term if merged is None else merged + term
    o_ref[...] = x + jnp.dot(merged.astype(BF16), wo_ref[...], preferred_element_type=F32)


def _merge(x2, g, wg, o_nsa, o_ret, o_conv, wb, wo):
    T = x2.shape[0]
    tm = 512
    br = pl.BlockSpec((tm, BRANCH_W), lambda i: (i, 0))
    return pl.pallas_call(
        _merge_body,
        grid=(T // tm,),
        in_specs=[pl.BlockSpec((tm, D_MODEL), lambda i: (i, 0)),
                  pl.BlockSpec((1, D_MODEL), lambda i: (0, 0)),
                  pl.BlockSpec(wg.shape, lambda i: (0, 0)),
                  pl.BlockSpec((BRANCH_W, tm), lambda i: (0, i)), br, br,
                  pl.BlockSpec(wb.shape, lambda i: (0, 0, 0)),
                  pl.BlockSpec(wo.shape, lambda i: (0, 0))],
        out_specs=pl.BlockSpec((tm, D_MODEL), lambda i: (i, 0)),
        out_shape=jax.ShapeDtypeStruct((T, D_MODEL), F32),
        compiler_params=_cparams(("parallel",)),
        name="merge_out_proj",
    )(x2, g, wg, o_nsa, o_ret, o_conv, wb, wo)


FF_CHUNK = 1024


def _ffn_body(x_ref, g_ref, w1_ref, w2_ref, gf_ref, o_ref, *, final_norm):
    x = x_ref[...]
    ms = jnp.mean(x * x, axis=-1, keepdims=True)
    h = (x * lax.rsqrt(ms + EPS) * g_ref[...]).astype(BF16)
    acc = x
    for j in range(D_FF // FF_CHUNK):
        cols = slice(j * FF_CHUNK, (j + 1) * FF_CHUNK)
        a = jnp.maximum(jnp.dot(h, w1_ref[:, cols], preferred_element_type=F32), 0.0)
        acc = acc + jnp.dot((a * a).astype(BF16), w2_ref[cols, :], preferred_element_type=F32)
    if final_norm:
        ms = jnp.mean(acc * acc, axis=-1, keepdims=True)
        acc = acc * lax.rsqrt(ms + EPS) * gf_ref[...]
    o_ref[...] = acc


def _ffn(x2, g, w1, w2, gf, final_norm):
    T = x2.shape[0]
    tm = 512
    vec = pl.BlockSpec((1, D_MODEL), lambda i: (0, 0))
    return pl.pallas_call(
        functools.partial(_ffn_body, final_norm=final_norm),
        grid=(T // tm,),
        in_specs=[pl.BlockSpec((tm, D_MODEL), lambda i: (i, 0)), vec,
                  pl.BlockSpec(w1.shape, lambda i: (0, 0)),
                  pl.BlockSpec(w2.shape, lambda i: (0, 0)), vec],
        out_specs=pl.BlockSpec((tm, D_MODEL), lambda i: (i, 0)),
        out_shape=jax.ShapeDtypeStruct((T, D_MODEL), F32),
        compiler_params=_cparams(("parallel",)),
        name="ffn",
    )(x2, g, w1, w2, gf)


def _rel_bucket(dist):
    n = jnp.maximum(dist, 0)
    max_exact = REL_BUCKETS // 2
    nf = jnp.maximum(n, 1).astype(F32)
    large = max_exact + (jnp.log(nf / max_exact) / math.log(REL_MAX_DIST / max_exact)
                         * (REL_BUCKETS - max_exact)).astype(jnp.int32)
    large = jnp.minimum(large, REL_BUCKETS - 1)
    return jnp.where(n < max_exact, n, large)


def _nsa_tables(rel_table, S):
    n_s = S // SEL_BLOCK
    n_cp = S // CMP_STRIDE
    QB, G, HPG = Q_BLOCK, NSA_GROUPS, NSA_HPG
    tab = rel_table.T.astype(F32) * LOG2E

    def bias(dist, ok=None):
        bucket = _rel_bucket(dist)[None]
        full = jnp.zeros((NSA_HEADS,) + dist.shape, F32)
        for b in range(REL_BUCKETS):
            full = jnp.where(bucket == b, tab[:, b][:, None, None], full)
        if ok is not None:
            full = jnp.where(ok[None], full, NEG)
        J = dist.shape[0]
        return jnp.transpose(full.reshape(G, HPG, J, QB), (0, 2, 1, 3)).reshape(G, J, HPG * QB)

    i = jnp.arange(QB)[None, :]
    et = (jnp.arange(S)[:, None] // SEL_BLOCK == jnp.arange(n_s)[None, :]).astype(BF16)
    cmp_start = jnp.arange(n_cp) * CMP_STRIDE
    sel_start = jnp.arange(n_s) * SEL_BLOCK
    ovt = ((cmp_start[None, :] <= sel_start[:, None] + SEL_BLOCK - 1)
           & (cmp_start[None, :] + CMP_BLOCK - 1 >= sel_start[:, None])
           & (jnp.arange(n_cp)[None, :] < n_cp - 1)).astype(BF16)
    d_near = i + (NEAR_TILES - 1) * QB - jnp.arange(NEAR_TILES * QB)[:, None]
    bnear = bias(d_near, d_near >= 0)
    cfar = bias(jnp.full((1, QB), S + REL_MAX_DIST, jnp.int32))
    d_win = i + WINDOW - jnp.arange(WINDOW + QB)[:, None]
    bwin = bias(d_win, (d_win >= 0) & (d_win < WINDOW))
    d_cmp = i - CMP_STRIDE * (jnp.arange(2 * n_cp)[:, None] - n_cp) - (CMP_BLOCK - 1)
    bcmp = bias(d_cmp, d_cmp >= 0)
    return et, ovt, bnear, cfar, bwin, bcmp


def _retention_tables(S):
    H, DK, C = RET_HEADS, RET_DK, RET_CHUNK
    half = DK // 2
    pos = jnp.arange(S, dtype=F32)
    inv = ROPE_BASE ** (-jnp.arange(half, dtype=F32) / half)
    ang = pos[:, None] * inv[None, :]
    cos = jnp.tile(jnp.cos(ang), (1, H))
    sin = jnp.tile(jnp.sin(ang), (1, H))
    log_g = jnp.log1p(-jnp.exp2(-5.0 - jnp.arange(H, dtype=F32)))
    ar = jnp.arange(C)
    diff = ar[:, None] - ar[None, :]
    decay = jnp.where(diff >= 0, jnp.exp(log_g[:, None, None] * jnp.maximum(diff, 0).astype(F32)), 0.0)
    zeta = jnp.exp(log_g[:, None] * (C - 1 - ar).astype(F32))
    xi = jnp.exp(log_g[:, None] * (ar + 1).astype(F32))
    gch = jnp.exp(log_g * C)
    lane_head = (jnp.arange(H * DK) % (H * half)) // half
    hm = (lane_head[None, :] == jnp.arange(H)[:, None]).astype(F32)
    return (gch, cos, sin, decay, hm[:, None, :], xi[:, :, None] * hm[:, None, :],
            zeta[:, :, None] * hm[:, None, :])


def _pack_w_in(w):
    o = np.cumsum([0, 512, 128, 128, 128, 128, 128, 128, 24, 256, 256, 512, 512, 512, 512, 3072])
    seg = lambda i: w[:, int(o[i]):int(o[i + 1])]

    def halves_major(a):
        a4 = a.reshape(a.shape[0], RET_HEADS, 2, RET_DK // 2)
        return jnp.transpose(a4, (0, 2, 1, 3)).reshape(a.shape[0], RET_HEADS * RET_DK)

    parts = [seg(12), seg(13), seg(0) * (NSA_DH ** -0.5 * LOG2E),
             seg(7), jnp.zeros((w.shape[0], LANE - 3 * NSA_HEADS), w.dtype), seg(10), seg(11),
             halves_major(seg(8)), halves_major(seg(9)) * (RET_DK ** -0.5),
             seg(3), seg(4), seg(5), seg(6), seg(1), seg(2)]
    return jnp.concatenate(parts, axis=1).astype(BF16), seg(14).astype(BF16)


def _pack_compress(pe, w1, w2):
    G, DH, r = NSA_GROUPS, NSA_DH, CMP_BLOCK // CMP_STRIDE
    eye = jnp.eye(G, dtype=w1.dtype)
    pe2 = jnp.tile(pe.reshape(r, CMP_STRIDE, 1, DH), (1, 1, G, 1)).reshape(r, CMP_STRIDE * G * DH)
    w1r = w1.reshape(r, CMP_STRIDE, DH, CMP_HIDDEN)
    w1b = jnp.einsum('rldf,gk->rlgdkf', w1r, eye).reshape(r, CMP_STRIDE * G * DH, G * CMP_HIDDEN)
    w2b = jnp.einsum('fd,gk->gfkd', w2, eye).reshape(G * CMP_HIDDEN, G * DH)
    return pe2, w1b.astype(BF16), w2b.astype(BF16)


def kernel(x, rel_table, norm_mix, w_in, cmp_pe_k, cmp_w1_k, cmp_w2_k, cmp_pe_v, cmp_w1_v, cmp_w2_v, ret_gn, conv_w, conv_b, conv_ln_g, conv_ln_b, w_branch, w_out, norm_mlp, w_ff1, w_ff2, norm_final):
    B, S, D = x.shape
    assert B % RET_SEQS == 0 and S % (2 * SEL_TILE) == 0
    depth = w_in.shape[0]
    T = B * S
    nsa_tabs = _nsa_tables(rel_table, S)
    ret_tabs = _retention_tables(S)
    x2 = x.reshape(T, D)
    for l in range(depth):
        w_main, w_gate = _pack_w_in(w_in[l])
        conv_w_pad = jnp.concatenate([conv_w[l], jnp.zeros((1, BRANCH_W), conv_w.dtype)], axis=0)
        z2, kc, vc, o_conv, qt, gnt = _in_proj(x2, norm_mix[l][None, :], w_main, S, conv_w_pad, conv_b[l][None, :],
                                      conv_ln_g[l][None, :], conv_ln_b[l][None, :])
        z3 = z2.reshape(B, S, Z_COLS)
        n_ch = S // CMP_STRIDE
        kc2 = kc.reshape(B, n_ch, CMP_STRIDE * NSA_KV)
        vc2 = vc.reshape(B, n_ch, CMP_STRIDE * NSA_KV)
        kcmp, vcmp = _compress(kc2, vc2, *_pack_compress(cmp_pe_k[l], cmp_w1_k[l], cmp_w2_k[l]),
                               *_pack_compress(cmp_pe_v[l], cmp_w1_v[l], cmp_w2_v[l]))
        kcmp = jnp.transpose(kcmp.reshape(B, n_ch, NSA_GROUPS, NSA_DH), (0, 2, 1, 3))
        vcmpt = jnp.transpose(vcmp.reshape(B, n_ch, NSA_GROUPS, NSA_DH), (0, 2, 3, 1))

        o_nsa_t = _nsa_attention(z3, qt, gnt, kcmp, vcmpt, nsa_tabs)
        o_ret = _retention(z3, ret_tabs, ret_gn[l][None, :])
        x2 = _merge(x2, norm_mix[l][None, :], w_gate, o_nsa_t, o_ret.reshape(T, BRANCH_W), o_conv.reshape(T, BRANCH_W),
                    w_branch[l].astype(BF16), w_out[l].astype(BF16))
        x2 = _ffn(x2, norm_mlp[l][None, :], w_ff1[l].astype(BF16), w_ff2[l].astype(BF16), norm_final[None, :],
                  final_norm=(l == depth - 1))
    return x2.reshape(B, S, D)
```

```python
import functools
import math

import jax
import jax.numpy as jnp
import numpy as np
from jax import lax
from jax.experimental import pallas as pl
from jax.experimental.pallas import tpu as pltpu

F32 = jnp.float32
BF16 = jnp.bfloat16

D_MODEL = 1024
BRANCH_W = 512
N_BRANCH = 3
NSA_HEADS = 8
NSA_GROUPS = 2
NSA_HPG = 4
NSA_DH = 64
NSA_KV = NSA_GROUPS * NSA_DH
CMP_BLOCK = 32
CMP_STRIDE = 16
CMP_HIDDEN = 128
SEL_BLOCK = 64
SEL_TOPK = 8
WINDOW = 512
Q_BLOCK = 128
RET_HEADS = 4
RET_DV = 128
RET_DK = 64
RET_CHUNK = 128
ROPE_BASE = 10000.0
CONV_WIDTH = 31
REL_BUCKETS = 32
REL_MAX_DIST = 128
D_FF = 4096
EPS = 1e-6
NEG = -1e30
LOG2E = 1.4426950408889634

LANE = 128
VMEM_LIMIT = 56 * 1024 * 1024

GLU_COLS = 2 * BRANCH_W
QT_COLS = BRANCH_W + LANE
COL_VR = 0
COL_GR = 512
COL_QR = 1024
COL_KR = 1280
COL_KS = 1536
Z_COLS = 2048
W_COLS = GLU_COLS + QT_COLS + Z_COLS + 2 * NSA_KV


def _cparams(sem):
    return pltpu.CompilerParams(dimension_semantics=sem, vmem_limit_bytes=VMEM_LIMIT)


def _rmsnorm_bf16(x, g):
    ms = jnp.mean(x * x, axis=-1, keepdims=True)
    return (x * lax.rsqrt(ms + EPS) * g).astype(BF16)


CONV_HALO = 32
SUBLANES = 8
CONV_ROWS = 64


def _conv_stage(u, u_scr):
    tile = u.shape[0]
    rows = CONV_HALO + tile
    u_scr[0, 0:CONV_HALO, :] = u_scr[0, tile:rows, :]
    u_scr[0, CONV_HALO:rows, :] = u
    for r in range(1, SUBLANES):
        u_scr[r, 0:rows - SUBLANES, :] = u_scr[0, r:rows - SUBLANES + r, :]


def _conv_rows(r0, w_ref, cb_ref, lg_ref, lb_ref, o_ref, u_scr):
    lead = CONV_HALO - (CONV_WIDTH - 1)
    acc = jnp.zeros((CONV_ROWS, u_scr.shape[2]), F32) + cb_ref[...]
    for k in range(CONV_WIDTH):
        start = r0 + (lead + k) // SUBLANES * SUBLANES
        acc = acc + w_ref[k:k + 1, :] * u_scr[(lead + k) % SUBLANES, pl.ds(start, CONV_ROWS), :]
    mu = jnp.mean(acc, axis=-1, keepdims=True)
    d = acc - mu
    var = jnp.mean(d * d, axis=-1, keepdims=True)
    y = d * lax.rsqrt(var + EPS) * lg_ref[...] + lb_ref[...]
    o_ref[pl.ds(r0, CONV_ROWS), :] = (y * jax.nn.sigmoid(y)).astype(o_ref.dtype)


def _in_proj_body(x_ref, g_ref, w_ref, cw_ref, cb_ref, lg_ref, lb_ref, z_ref, kc_ref, vc_ref, oc_ref, qt_ref, gnt_ref,
                  u_scr, *, tiles_per_seq):
    tm = x_ref.shape[0]

    @pl.when(pl.program_id(0) % tiles_per_seq == 0)
    def _():
        u_scr[0, tm:tm + CONV_HALO, :] = jnp.zeros((CONV_HALO, u_scr.shape[2]), F32)

    h = _rmsnorm_bf16(x_ref[...], g_ref[...])
    glu = jnp.dot(h, w_ref[:, 0:GLU_COLS], preferred_element_type=F32)
    acc = jnp.dot(h, w_ref[:, GLU_COLS:W_COLS], preferred_element_type=F32)
    qt_ref[...] = acc[:, 0:BRANCH_W].T.astype(qt_ref.dtype)
    gnt_ref[...] = acc[:, BRANCH_W:QT_COLS].T
    z_ref[...] = acc[:, QT_COLS:QT_COLS + Z_COLS]
    kc_ref[...] = acc[:, QT_COLS + Z_COLS:QT_COLS + Z_COLS + NSA_KV]
    vc_ref[...] = acc[:, QT_COLS + Z_COLS + NSA_KV:QT_COLS + Z_COLS + 2 * NSA_KV]
    _conv_stage(glu[:, 0:BRANCH_W] * jax.nn.sigmoid(glu[:, BRANCH_W:GLU_COLS]), u_scr)
    for r0 in range(0, tm, CONV_ROWS):
        _conv_rows(r0, cw_ref, cb_ref, lg_ref, lb_ref, oc_ref, u_scr)


def _in_proj(x2, g, w, seq_len, conv_w, conv_b, conv_lg, conv_lb):
    T = x2.shape[0]
    tm = 512
    assert seq_len % tm == 0

    def const(a):
        return pl.BlockSpec(a.shape, lambda i: (0, 0))

    def rows(width):
        return pl.BlockSpec((tm, width), lambda i: (i, 0))

    return pl.pallas_call(
        functools.partial(_in_proj_body, tiles_per_seq=seq_len // tm),
        grid=(T // tm,),
        in_specs=[rows(D_MODEL), const(g), const(w), const(conv_w), const(conv_b), const(conv_lg), const(conv_lb)],
        out_specs=[rows(Z_COLS), rows(NSA_KV), rows(NSA_KV), rows(BRANCH_W),
                   pl.BlockSpec((BRANCH_W, tm), lambda i: (0, i)), pl.BlockSpec((LANE, tm), lambda i: (0, i))],
        out_shape=[jax.ShapeDtypeStruct((T, Z_COLS), F32), jax.ShapeDtypeStruct((T, NSA_KV), F32),
                   jax.ShapeDtypeStruct((T, NSA_KV), F32), jax.ShapeDtypeStruct((T, BRANCH_W), BF16),
                   jax.ShapeDtypeStruct((BRANCH_W, T), BF16), jax.ShapeDtypeStruct((LANE, T), F32)],
        scratch_shapes=[pltpu.VMEM((SUBLANES, CONV_HALO + tm, BRANCH_W), F32)],
        compiler_params=_cparams(("arbitrary",)),
        name="in_proj_conv",
    )(x2, g, w, conv_w, conv_b, conv_lg, conv_lb)


def _compress_body(kc_ref, vc_ref, pek_ref, w1k_ref, w2k_ref, pev_ref, w1v_ref, w2v_ref, ko_ref, vo_ref):
    def one(c_ref, pe_ref, w1_ref, w2_ref, o_ref):
        ch = c_ref[...]
        a = jnp.dot((ch + pe_ref[0:1, :]).astype(BF16), w1_ref[0], preferred_element_type=F32)
        b = jnp.dot((ch + pe_ref[1:2, :]).astype(BF16), w1_ref[1], preferred_element_type=F32)
        n = b.shape[0]
        hid = jax.nn.gelu(a + pltpu.roll(b, n - 1, 0))
        o_ref[...] = jnp.dot(hid.astype(BF16), w2_ref[...], preferred_element_type=F32).astype(o_ref.dtype)

    one(kc_ref, pek_ref, w1k_ref, w2k_ref, ko_ref)
    one(vc_ref, pev_ref, w1v_ref, w2v_ref, vo_ref)


def _compress(kc2, vc2, pek, w1k, w2k, pev, w1v, w2v):
    B, n_ch, wide = kc2.shape
    hid2 = NSA_GROUPS * CMP_HIDDEN
    act = pl.BlockSpec((None, n_ch, wide), lambda b: (b, 0, 0))
    pe = pl.BlockSpec((2, wide), lambda b: (0, 0))
    w1 = pl.BlockSpec((2, wide, hid2), lambda b: (0, 0, 0))
    w2 = pl.BlockSpec((hid2, NSA_KV), lambda b: (0, 0))
    out = pl.BlockSpec((None, n_ch, NSA_KV), lambda b: (b, 0, 0))
    return pl.pallas_call(
        _compress_body,
        grid=(B,),
        in_specs=[act, act, pe, w1, w2, pe, w1, w2],
        out_specs=[out, out],
        out_shape=[jax.ShapeDtypeStruct((B, n_ch, NSA_KV), BF16)] * 2,
        compiler_params=_cparams(("parallel",)),
        name="nsa_compress",
    )(kc2, vc2, pek, w1k, w2k, pev, w1v, w2v)


SEL_TILE = 2 * Q_BLOCK
V_ROWS = NSA_DH + 16
SOFTMAX_ROWS = 64
NEAR_TILES = 3


def _nsa_body(q_ref, gn_ref, ks_ref, vs_ref, kw_ref, vw_ref, kcmp_ref, vcmpt_ref,
              et_ref, ovt_ref, bnear_ref, cfar_ref, bwin_ref, bcmp_ref, o_ref,
              kaug_scr, kw_scr, vst_ref, vwt_ref, part_scr, near_scr, sa_scr, sb_scr, pa_scr, pb_scr):
    c = pl.program_id(1)
    n_s = et_ref.shape[1]
    n_cp = kcmp_ref.shape[1]
    QB, DH, HPG = Q_BLOCK, NSA_DH, NSA_HPG
    s_len = ks_ref.shape[0]

    @pl.when(c == 0)
    def _():
        for g in range(NSA_GROUPS):
            kaug_scr[g, 0:s_len, 0:DH] = ks_ref[:, g * DH:(g + 1) * DH].astype(BF16)
            kaug_scr[g, 0:s_len, DH:DH + n_s] = et_ref[...]
            kaug_scr[g, s_len:s_len + SEL_TILE, 0:DH] = jnp.zeros((SEL_TILE, DH), BF16)
            kaug_scr[g, s_len:s_len + SEL_TILE, DH:DH + n_s] = jnp.ones((SEL_TILE, n_s), BF16)
            kw_scr[g, 0:WINDOW, :] = (lax.broadcasted_iota(jnp.int32, (WINDOW, LANE), 1) == DH).astype(BF16)
            kw_scr[g, WINDOW:WINDOW + s_len, 0:DH] = kw_ref[:, g * DH:(g + 1) * DH].astype(BF16)
            kw_scr[g, WINDOW:WINDOW + s_len, DH:LANE] = jnp.zeros((s_len, LANE - DH), BF16)
            vwt_ref[g, :, 0:WINDOW] = jnp.zeros((V_ROWS, WINDOW), BF16)

        def transpose_values(t, _):
            r0 = pl.multiple_of(t * QB, QB)
            for src, dst, pad in ((vs_ref, vst_ref, 0), (vw_ref, vwt_ref, WINDOW)):
                v_t = src[pl.ds(r0, QB), :].T.astype(BF16)
                for g in range(NSA_GROUPS):
                    dst[g, 0:DH, pl.ds(pad + r0, QB)] = v_t[g * DH:(g + 1) * DH]
            return 0

        lax.fori_loop(0, s_len // QB, transpose_values, 0)
        ones_row = (lax.broadcasted_iota(jnp.int32, (V_ROWS - DH, s_len), 0) == 0).astype(BF16)
        for g in range(NSA_GROUPS):
            vst_ref[g, DH:V_ROWS, :] = ones_row
            vwt_ref[g, DH:V_ROWS, WINDOW:WINDOW + s_len] = ones_row

    gates = jax.nn.sigmoid(gn_ref[...])

    blk = lax.broadcasted_iota(jnp.int32, (n_s, QB), 0)
    cur = (c * QB + lax.broadcasted_iota(jnp.int32, (n_s, QB), 1)) // SEL_BLOCK
    valid = blk <= cur
    forced = (blk == 0) | (blk == cur) | (blk == cur - 1)
    blk_f = blk.astype(F32)

    cmp_off = pl.multiple_of(n_cp - c * (QB // CMP_STRIDE), QB // CMP_STRIDE)
    n_far = jnp.maximum(c - 1, 0)
    n_w = WINDOW // QB + 1

    def normalised(acc):
        return acc[0:DH] * (1.0 / jnp.maximum(acc[DH:DH + 1], 1e-30))

    def gate_row(g, branch):
        return jnp.concatenate([gates[3 * (g * HPG + h) + branch:3 * (g * HPG + h) + branch + 1, :]
                                for h in range(HPG)], axis=1)

    near_tiles = ((n_far - 1, n_far % 2 == 1), (c - 1, c >= 1), (c, None))
    near_rows = [pl.multiple_of(t * QB if ok is None else jnp.where(ok, t * QB, s_len), QB) for t, ok in near_tiles]
    near_cols = [pl.multiple_of(jnp.maximum(t, 0) * QB, QB) for t, _ in near_tiles]
    win_flag = jnp.where(lax.broadcasted_iota(jnp.int32, (LANE - DH, HPG * QB), 0) == 0, NEG, 0.0).astype(BF16)
    w0 = pl.multiple_of(c * QB, QB)
    groups = range(NSA_GROUPS)

    qts = [jnp.concatenate([q_ref[(g * HPG + h) * DH:(g * HPG + h + 1) * DH, :] for h in range(HPG)], axis=1)
           for g in groups]
    s_cs = [jnp.dot(kcmp_ref[g], qts[g], preferred_element_type=F32) for g in groups]
    s_ws = [jnp.dot(kw_scr[g, pl.ds(w0, WINDOW + QB), :], jnp.concatenate([qts[g], win_flag], axis=0),
                    preferred_element_type=F32) for g in groups]

    qas, o_cs = [], []
    for g in groups:
        s_c = s_cs[g] + bcmp_ref[g, pl.ds(cmp_off, n_cp), :]
        e_c = jnp.exp2(s_c - jnp.maximum(jnp.max(s_c, axis=0, keepdims=True), 0.1 * NEG))
        p_c = e_c * (1.0 / jnp.maximum(jnp.sum(e_c, axis=0, keepdims=True), 1e-30))
        o_cs.append(jnp.dot(vcmpt_ref[g], p_c.astype(BF16), preferred_element_type=F32))

        p_sum = p_c[:, 0:QB] + p_c[:, QB:2 * QB] + p_c[:, 2 * QB:3 * QB] + p_c[:, 3 * QB:4 * QB]
        p_hi = p_sum.astype(BF16)
        p_lo = (p_sum - p_hi.astype(F32)).astype(BF16)
        imp = (jnp.dot(ovt_ref[...], p_hi, preferred_element_type=F32)
               + jnp.dot(ovt_ref[...], p_lo, preferred_element_type=F32))
        score = jnp.where(forced, imp + 1e4, jnp.where(valid, imp, -1e4))
        selneg = jnp.full((n_s, QB), NEG, F32)
        for _ in range(min(SEL_TOPK, n_s)):
            m = jnp.max(score, axis=0, keepdims=True)
            first = jnp.min(jnp.where(score == m, blk_f, float(n_s)), axis=0, keepdims=True)
            pick = blk_f == first
            selneg = jnp.where(pick, 0.0, selneg)
            score = jnp.where(pick, -jnp.inf, score)
        qas.append(jnp.concatenate([qts[g], jnp.concatenate([selneg.astype(BF16)] * HPG, axis=1)], axis=0))

    s_ns = [jnp.dot(jnp.concatenate([kaug_scr[g, pl.ds(r0, QB), :] for r0 in near_rows], axis=0), qas[g],
                    preferred_element_type=F32) for g in groups]

    for g in groups:
        s_w = s_ws[g] + bwin_ref[g]
        e_w = jnp.exp2(s_w - jnp.max(s_w, axis=0, keepdims=True))
        o_w = normalised(jnp.dot(vwt_ref[g, :, pl.ds(w0, WINDOW + QB)], e_w.astype(BF16),
                                 preferred_element_type=F32))
        part_scr[g] = gate_row(g, 0) * o_cs[g] + gate_row(g, 2) * o_w

    near_m = []
    for g in groups:
        s_n = s_ns[g] + bnear_ref[g]
        m_n = jnp.max(s_n, axis=0, keepdims=True)
        near_m.append(m_n)
        v_n = jnp.concatenate([vst_ref[g, :, pl.ds(r0, QB)] for r0 in near_cols], axis=1)
        near_scr[g] = jnp.dot(v_n, jnp.exp2(s_n - m_n).astype(BF16), preferred_element_type=F32)

    n_it = n_far // 2

    def scores(kt, g, dst):
        r0 = pl.multiple_of(jnp.where(kt < n_it, kt * SEL_TILE, s_len), SEL_TILE)
        dst[g] = jnp.dot(kaug_scr[g, pl.ds(r0, SEL_TILE), :], qas[g], preferred_element_type=F32)

    def values(kt, g, p_src):
        r0 = pl.multiple_of(jnp.clip(kt, 0, jnp.maximum(n_it - 1, 0)) * SEL_TILE, SEL_TILE)
        return jnp.dot(vst_ref[g, :, pl.ds(r0, SEL_TILE)], p_src[g], preferred_element_type=F32)

    def half(kt, carries, src, dst, p_src, p_dst):
        if dst is not None:
            for g in groups:
                scores(kt + 1, g, dst)
        pv = [values(kt - 1, g, p_src) for g in groups]
        out = []
        for g in groups:
            m_i, acc, alpha_prev = carries[g]
            m_n = m_i
            for r0 in range(0, SEL_TILE, SOFTMAX_ROWS):
                m_n = jnp.maximum(m_n, jnp.max(src[g, r0:r0 + SOFTMAX_ROWS, :], axis=0, keepdims=True))
            for r0 in range(0, SEL_TILE, SOFTMAX_ROWS):
                p_dst[g, r0:r0 + SOFTMAX_ROWS, :] = jnp.exp2(src[g, r0:r0 + SOFTMAX_ROWS, :] - m_n).astype(BF16)
            out.append((m_n, alpha_prev * acc + pv[g], jnp.exp2(m_i - m_n)))
        return tuple(out)

    def far(j, carries):
        first = half(2 * j, carries, sa_scr, sb_scr, pb_scr, pa_scr)
        return half(2 * j + 1, first, sb_scr, sa_scr, pa_scr, pb_scr)

    for g in groups:
        pb_scr[g] = jnp.zeros((SEL_TILE, HPG * QB), BF16)
        scores(0, g, sa_scr)
    init = (jnp.full((1, HPG * QB), NEG, F32), jnp.zeros((V_ROWS, HPG * QB), F32), jnp.ones((1, HPG * QB), F32))
    n_trips = n_it // 2
    piped = lax.fori_loop(0, n_trips, far, (init,) * NSA_GROUPS)

    def drain(kt, p_src, state):
        return tuple((m_i, alpha_prev * acc + values(kt, g, p_src)) for g, (m_i, acc, alpha_prev) in enumerate(state))

    def odd_tail(state):
        return drain(2 * n_trips, pa_scr, half(2 * n_trips, state, sa_scr, None, pb_scr, pa_scr))

    def even_tail(state):
        return drain(2 * n_trips - 1, pb_scr, state)

    carries = lax.cond(n_it % 2 == 1, odd_tail, even_tail, piped)

    for g in range(NSA_GROUPS):
        m_f, acc_f = carries[g]
        m_f = m_f + cfar_ref[g]
        m_s = jnp.maximum(m_f, near_m[g])
        acc_s = jnp.exp2(m_f - m_s) * acc_f + jnp.exp2(near_m[g] - m_s) * near_scr[g]
        o_g = part_scr[g] + gate_row(g, 1) * normalised(acc_s)
        for h in range(HPG):
            hh = g * HPG + h
            o_ref[hh * DH:(hh + 1) * DH, :] = o_g[:, h * QB:(h + 1) * QB].astype(o_ref.dtype)


def _nsa_attention(z3, qt, gnt, kcmp, vcmpt, tabs):
    B, S, _ = z3.shape
    n_qb = S // Q_BLOCK
    et, ovt, bnear, cfar, bwin, bcmp = tabs

    def tcol(rows):
        return pl.BlockSpec((rows, Q_BLOCK), lambda b, c: (0, b * n_qb + c))

    def zfull(col):
        return pl.BlockSpec((None, S, NSA_KV), lambda b, c: (b, 0, col // NSA_KV))

    def per_batch(a):
        nd = a.ndim - 1
        return pl.BlockSpec((None,) + a.shape[1:], lambda b, c: (b,) + (0,) * nd)

    def const(a):
        nd = a.ndim
        return pl.BlockSpec(a.shape, lambda b, c: (0,) * nd)

    tile = (NSA_GROUPS, SEL_TILE, NSA_HPG * Q_BLOCK)
    return pl.pallas_call(
        _nsa_body,
        grid=(B, S // Q_BLOCK),
        in_specs=[tcol(BRANCH_W), tcol(LANE),
                  zfull(COL_KS), zfull(COL_KS + NSA_KV), zfull(COL_KS + 2 * NSA_KV), zfull(COL_KS + 3 * NSA_KV),
                  per_batch(kcmp), per_batch(vcmpt),
                  const(et), const(ovt), const(bnear), const(cfar), const(bwin), const(bcmp)],
        out_specs=tcol(BRANCH_W),
        out_shape=jax.ShapeDtypeStruct((BRANCH_W, B * S), BF16),
        scratch_shapes=[pltpu.VMEM((NSA_GROUPS, S + SEL_TILE, NSA_DH + et.shape[1]), BF16),
                        pltpu.VMEM((NSA_GROUPS, WINDOW + S, LANE), BF16),
                        pltpu.VMEM((NSA_GROUPS, V_ROWS, S), BF16), pltpu.VMEM((NSA_GROUPS, V_ROWS, WINDOW + S), BF16),
                        pltpu.VMEM((NSA_GROUPS, NSA_DH, NSA_HPG * Q_BLOCK), F32),
                        pltpu.VMEM((NSA_GROUPS, V_ROWS, NSA_HPG * Q_BLOCK), F32),
                        pltpu.VMEM(tile, F32), pltpu.VMEM(tile, F32), pltpu.VMEM(tile, BF16), pltpu.VMEM(tile, BF16)],
        compiler_params=_cparams(("parallel", "arbitrary")),
        name="nsa_attention",
    )(qt, gnt, z3, z3, z3, z3, kcmp, vcmpt, et, ovt, bnear, cfar, bwin, bcmp)


RET_SEQS = 4


def _retention_body(gch_ref, q_ref, k_ref, v_ref, g_ref, cos_ref, sin_ref, decay_ref, hm_ref, xi_ref,
                    zeta_ref, gn_ref, o_ref, state_scr):
    @pl.when(pl.program_id(1) == 0)
    def _():
        state_scr[...] = jnp.zeros_like(state_scr)

    half = RET_HEADS * RET_DK // 2
    cos, sin = cos_ref[...], sin_ref[...]

    def rot(x):
        x1, x2 = x[:, :half], x[:, half:]
        return jnp.concatenate([x1 * cos - x2 * sin, x1 * sin + x2 * cos], axis=1)

    nt = (((1,), (1,)), ((), ()))
    tn = (((0,), (0,)), ((), ()))
    pairs = [(r, h) for r in range(RET_SEQS) for h in range(RET_HEADS)]
    qr = [rot(q_ref[r]) for r in range(RET_SEQS)]
    kr = [rot(k_ref[r]) for r in range(RET_SEQS)]
    kb = [x.astype(BF16) for x in kr]
    vh = {(r, h): v_ref[r, :, h * RET_DV:(h + 1) * RET_DV].astype(BF16) for r, h in pairs}
    inner = {(r, h): lax.dot_general((qr[r] * hm_ref[h]).astype(BF16), kb[r], nt, preferred_element_type=F32)
             for r, h in pairs}
    state = {(r, h): state_scr[r, h] for r, h in pairs}
    cross = {(r, h): jnp.dot((qr[r] * xi_ref[h]).astype(BF16), state[r, h].astype(BF16),
                             preferred_element_type=F32) for r, h in pairs}
    kv = {(r, h): lax.dot_general((kr[r] * zeta_ref[h]).astype(BF16), vh[r, h], tn, preferred_element_type=F32)
          for r, h in pairs}
    for r, h in pairs:
        state_scr[r, h] = gch_ref[h] * state[r, h] + kv[r, h]
    for r, h in pairs:
        cols = slice(h * RET_DV, (h + 1) * RET_DV)
        o = jnp.dot((inner[r, h] * decay_ref[h]).astype(BF16), vh[r, h], preferred_element_type=F32) + cross[r, h]
        mu = jnp.mean(o, axis=-1, keepdims=True)
        d = o - mu
        var = jnp.mean(d * d, axis=-1, keepdims=True)
        y = d * lax.rsqrt(var + EPS) * gn_ref[:, cols]
        gate = g_ref[r, :, cols]
        o_ref[r, :, cols] = (gate * jax.nn.sigmoid(gate) * y).astype(o_ref.dtype)


def _retention(z3, tabs, gn):
    B, S, _ = z3.shape
    C = RET_CHUNK
    gch, cos, sin, decay, hm, xi, zeta = tabs
    qk_w = RET_HEADS * RET_DK

    def zcol(width, col):
        return pl.BlockSpec((RET_SEQS, C, width), lambda b, n: (b, n, col // width))

    def const(a):
        nd = a.ndim
        return pl.BlockSpec(a.shape, lambda b, n: (0,) * nd)

    pos = pl.BlockSpec((C, qk_w // 2), lambda b, n: (n, 0))
    return pl.pallas_call(
        _retention_body,
        grid=(B // RET_SEQS, S // C),
        in_specs=[pl.BlockSpec(memory_space=pltpu.SMEM),
                  zcol(qk_w, COL_QR), zcol(qk_w, COL_KR), zcol(BRANCH_W, COL_VR), zcol(BRANCH_W, COL_GR),
                  pos, pos, const(decay), const(hm), const(xi), const(zeta), const(gn)],
        out_specs=pl.BlockSpec((RET_SEQS, C, BRANCH_W), lambda b, n: (b, n, 0)),
        out_shape=jax.ShapeDtypeStruct((B, S, BRANCH_W), BF16),
        scratch_shapes=[pltpu.VMEM((RET_SEQS, RET_HEADS, qk_w, RET_DV), F32)],
        compiler_params=_cparams(("parallel", "arbitrary")),
        name="retention",
    )(gch, z3, z3, z3, z3, cos, sin, decay, hm, xi, zeta, gn)


def _merge_body(x_ref, g_ref, wg_ref, a_ref, r_ref, c_ref, wb_ref, wo_ref, o_ref):
    x = x_ref[...]
    h = _rmsnorm_bf16(x, g_ref[...])
    merged = None
    nsa = a_ref[...].astype(F32).T.astype(BF16)
    for i, br in enumerate((nsa, r_ref[...], c_ref[...])):
        gate = jnp.dot(h, wg_ref[:, i * D_MODEL:(i + 1) * D_MODEL], preferred_element_type=F32)
        proj = jnp.dot(br, wb_ref[i], preferred_element_type=F32)
        term = jax.nn.sigmoid(gate) * proj
        merged = term if merged is None else merged + term
    o_ref[...] = x + jnp.dot(merged.astype(BF16), wo_ref[...], preferred_element_type=F32)


def _merge(x2, g, wg, o_nsa, o_ret, o_conv, wb, wo):
    T = x2.shape[0]
    tm = 512
    br = pl.BlockSpec((tm, BRANCH_W), lambda i: (i, 0))
    return pl.pallas_call(
        _merge_body,
        grid=(T // tm,),
        in_specs=[pl.BlockSpec((tm, D_MODEL), lambda i: (i, 0)),
                  pl.BlockSpec((1, D_MODEL), lambda i: (0, 0)),
                  pl.BlockSpec(wg.shape, lambda i: (0, 0)),
                  pl.BlockSpec((BRANCH_W, tm), lambda i: (0, i)), br, br,
                  pl.BlockSpec(wb.shape, lambda i: (0, 0, 0)),
                  pl.BlockSpec(wo.shape, lambda i: (0, 0))],
        out_specs=pl.BlockSpec((tm, D_MODEL), lambda i: (i, 0)),
        out_shape=jax.ShapeDtypeStruct((T, D_MODEL), F32),
        compiler_params=_cparams(("parallel",)),
        name="merge_out_proj",
    )(x2, g, wg, o_nsa, o_ret, o_conv, wb, wo)


FF_CHUNK = 1024


def _ffn_body(x_ref, g_ref, w1_ref, w2_ref, gf_ref, o_ref, *, final_norm):
    x = x_ref[...]
    ms = jnp.mean(x * x, axis=-1, keepdims=True)
    h = (x * lax.rsqrt(ms + EPS) * g_ref[...]).astype(BF16)
    acc = x
    for j in range(D_FF // FF_CHUNK):
        cols = slice(j * FF_CHUNK, (j + 1) * FF_CHUNK)
        a = jnp.maximum(jnp.dot(h, w1_ref[:, cols], preferred_element_type=F32), 0.0)
        acc = acc + jnp.dot((a * a).astype(BF16), w2_ref[cols, :], preferred_element_type=F32)
    if final_norm:
        ms = jnp.mean(acc * acc, axis=-1, keepdims=True)
        acc = acc * lax.rsqrt(ms + EPS) * gf_ref[...]
    o_ref[...] = acc


def _ffn(x2, g, w1, w2, gf, final_norm):
    T = x2.shape[0]
    tm = 512
    vec = pl.BlockSpec((1, D_MODEL), lambda i: (0, 0))
    return pl.pallas_call(
        functools.partial(_ffn_body, final_norm=final_norm),
        grid=(T // tm,),
        in_specs=[pl.BlockSpec((tm, D_MODEL), lambda i: (i, 0)), vec,
                  pl.BlockSpec(w1.shape, lambda i: (0, 0)),
                  pl.BlockSpec(w2.shape, lambda i: (0, 0)), vec],
        out_specs=pl.BlockSpec((tm, D_MODEL), lambda i: (i, 0)),
        out_shape=jax.ShapeDtypeStruct((T, D_MODEL), F32),
        compiler_params=_cparams(("parallel",)),
        name="ffn",
    )(x2, g, w1, w2, gf)


def _rel_bucket(dist):
    n = jnp.maximum(dist, 0)
    max_exact = REL_BUCKETS // 2
    nf = jnp.maximum(n, 1).astype(F32)
    large = max_exact + (jnp.log(nf / max_exact) / math.log(REL_MAX_DIST / max_exact)
                         * (REL_BUCKETS - max_exact)).astype(jnp.int32)
    large = jnp.minimum(large, REL_BUCKETS - 1)
    return jnp.where(n < max_exact, n, large)


def _nsa_tables(rel_table, S):
    n_s = S // SEL_BLOCK
    n_cp = S // CMP_STRIDE
    QB, G, HPG = Q_BLOCK, NSA_GROUPS, NSA_HPG
    tab = rel_table.T.astype(F32) * LOG2E

    def bias(dist, ok=None):
        bucket = _rel_bucket(dist)[None]
        full = jnp.zeros((NSA_HEADS,) + dist.shape, F32)
        for b in range(REL_BUCKETS):
            full = jnp.where(bucket == b, tab[:, b][:, None, None], full)
        if ok is not None:
            full = jnp.where(ok[None], full, NEG)
        J = dist.shape[0]
        return jnp.transpose(full.reshape(G, HPG, J, QB), (0, 2, 1, 3)).reshape(G, J, HPG * QB)

    i = jnp.arange(QB)[None, :]
    et = (jnp.arange(S)[:, None] // SEL_BLOCK == jnp.arange(n_s)[None, :]).astype(BF16)
    cmp_start = jnp.arange(n_cp) * CMP_STRIDE
    sel_start = jnp.arange(n_s) * SEL_BLOCK
    ovt = ((cmp_start[None, :] <= sel_start[:, None] + SEL_BLOCK - 1)
           & (cmp_start[None, :] + CMP_BLOCK - 1 >= sel_start[:, None])
           & (jnp.arange(n_cp)[None, :] < n_cp - 1)).astype(BF16)
    d_near = i + (NEAR_TILES - 1) * QB - jnp.arange(NEAR_TILES * QB)[:, None]
    bnear = bias(d_near, d_near >= 0)
    cfar = bias(jnp.full((1, QB), S + REL_MAX_DIST, jnp.int32))
    d_win = i + WINDOW - jnp.arange(WINDOW + QB)[:, None]
    bwin = bias(d_win, (d_win >= 0) & (d_win < WINDOW))
    d_cmp = i - CMP_STRIDE * (jnp.arange(2 * n_cp)[:, None] - n_cp) - (CMP_BLOCK - 1)
    bcmp = bias(d_cmp, d_cmp >= 0)
    return et, ovt, bnear, cfar, bwin, bcmp


def _retention_tables(S):
    H, DK, C = RET_HEADS, RET_DK, RET_CHUNK
    half = DK // 2
    pos = jnp.arange(S, dtype=F32)
    inv = ROPE_BASE ** (-jnp.arange(half, dtype=F32) / half)
    ang = pos[:, None] * inv[None, :]
    cos = jnp.tile(jnp.cos(ang), (1, H))
    sin = jnp.tile(jnp.sin(ang), (1, H))
    log_g = jnp.log1p(-jnp.exp2(-5.0 - jnp.arange(H, dtype=F32)))
    ar = jnp.arange(C)
    diff = ar[:, None] - ar[None, :]
    decay = jnp.where(diff >= 0, jnp.exp(log_g[:, None, None] * jnp.maximum(diff, 0).astype(F32)), 0.0)
    zeta = jnp.exp(log_g[:, None] * (C - 1 - ar).astype(F32))
    xi = jnp.exp(log_g[:, None] * (ar + 1).astype(F32))
    gch = jnp.exp(log_g * C)
    lane_head = (jnp.arange(H * DK) % (H * half)) // half
    hm = (lane_head[None, :] == jnp.arange(H)[:, None]).astype(F32)
    return (gch, cos, sin, decay, hm[:, None, :], xi[:, :, None] * hm[:, None, :],
            zeta[:, :, None] * hm[:, None, :])


def _pack_w_in(w):
    o = np.cumsum([0, 512, 128, 128, 128, 128, 128, 128, 24, 256, 256, 512, 512, 512, 512, 3072])
    seg = lambda i: w[:, int(o[i]):int(o[i + 1])]

    def halves_major(a):
        a4 = a.reshape(a.shape[0], RET_HEADS, 2, RET_DK // 2)
        return jnp.transpose(a4, (0, 2, 1, 3)).reshape(a.shape[0], RET_HEADS * RET_DK)

    parts = [seg(12), seg(13), seg(0) * (NSA_DH ** -0.5 * LOG2E),
             seg(7), jnp.zeros((w.shape[0], LANE - 3 * NSA_HEADS), w.dtype), seg(10), seg(11),
             halves_major(seg(8)), halves_major(seg(9)) * (RET_DK ** -0.5),
             seg(3), seg(4), seg(5), seg(6), seg(1), seg(2)]
    return jnp.concatenate(parts, axis=1).astype(BF16), seg(14).astype(BF16)


def _pack_compress(pe, w1, w2):
    G, DH, r = NSA_GROUPS, NSA_DH, CMP_BLOCK // CMP_STRIDE
    eye = jnp.eye(G, dtype=w1.dtype)
    pe2 = jnp.tile(pe.reshape(r, CMP_STRIDE, 1, DH), (1, 1, G, 1)).reshape(r, CMP_STRIDE * G * DH)
    w1r = w1.reshape(r, CMP_STRIDE, DH, CMP_HIDDEN)
    w1b = jnp.einsum('rldf,gk->rlgdkf', w1r, eye).reshape(r, CMP_STRIDE * G * DH, G * CMP_HIDDEN)
    w2b = jnp.einsum('fd,gk->gfkd', w2, eye).reshape(G * CMP_HIDDEN, G * DH)
    return pe2, w1b.astype(BF16), w2b.astype(BF16)


def kernel(x, rel_table, norm_mix, w_in, cmp_pe_k, cmp_w1_k, cmp_w2_k, cmp_pe_v, cmp_w1_v, cmp_w2_v, ret_gn, conv_w, conv_b, conv_ln_g, conv_ln_b, w_branch, w_out, norm_mlp, w_ff1, w_ff2, norm_final):
    B, S, D = x.shape
    assert B % RET_SEQS == 0 and S % (2 * SEL_TILE) == 0
    depth = w_in.shape[0]
    T = B * S
    nsa_tabs = _nsa_tables(rel_table, S)
    ret_tabs = _retention_tables(S)
    x2 = x.reshape(T, D)
    for l in range(depth):
        w_main, w_gate = _pack_w_in(w_in[l])
        conv_w_pad = jnp.concatenate([conv_w[l], jnp.zeros((1, BRANCH_W), conv_w.dtype)], axis=0)
        z2, kc, vc, o_conv, qt, gnt = _in_proj(x2, norm_mix[l][None, :], w_main, S, conv_w_pad, conv_b[l][None, :],
                                      conv_ln_g[l][None, :], conv_ln_b[l][None, :])
        z3 = z2.reshape(B, S, Z_COLS)
        n_ch = S // CMP_STRIDE
        kc2 = kc.reshape(B, n_ch, CMP_STRIDE * NSA_KV)
        vc2 = vc.reshape(B, n_ch, CMP_STRIDE * NSA_KV)
        kcmp, vcmp = _compress(kc2, vc2, *_pack_compress(cmp_pe_k[l], cmp_w1_k[l], cmp_w2_k[l]),
                               *_pack_compress(cmp_pe_v[l], cmp_w1_v[l], cmp_w2_v[l]))
        kcmp = jnp.transpose(kcmp.reshape(B, n_ch, NSA_GROUPS, NSA_DH), (0, 2, 1, 3))
        vcmpt = jnp.transpose(vcmp.reshape(B, n_ch, NSA_GROUPS, NSA_DH), (0, 2, 3, 1))

        o_nsa_t = _nsa_attention(z3, qt, gnt, kcmp, vcmpt, nsa_tabs)
        o_ret = _retention(z3, ret_tabs, ret_gn[l][None, :])
        x2 = _merge(x2, norm_mix[l][None, :], w_gate, o_nsa_t, o_ret.reshape(T, BRANCH_W), o_conv.reshape(T, BRANCH_W),
                    w_branch[l].astype(BF16), w_out[l].astype(BF16))
        x2 = _ffn(x2, norm_mlp[l][None, :], w_ff1[l].astype(BF16), w_ff2[l].astype(BF16), norm_final[None, :],
                  final_norm=(l == depth - 1))
    return x2.reshape(B, S, D)
```
